```python
import math
import jax
import jax.numpy as jnp
from jax import lax
import numpy as np


D_MODEL = 1024
BATCH = 4
SEQ = 8192
DEPTH = 4

CTX_LEN = 256
GRID_W = 64
RET_HEADS = 4
RET_DK = 128
RET_DV = 128
RET_W = RET_HEADS * RET_DV
RET_CHUNK = 128
NA_HEADS = 8
NA_DH = 64
NA_W = NA_HEADS * NA_DH
NA_WIN_ROWS = 8
NA_WIN_COLS = 16
NA_ROW_BLOCK = 8
GQA_HEADS = 8
GQA_KV_HEADS = 2
GQA_DH = 64
GQA_W = GQA_HEADS * GQA_DH
GQA_KV_W = GQA_KV_HEADS * GQA_DH
Q_BLOCK = 128
ROPE_THETA = 10000.0
EPS = 1e-6
N_BRANCH = 3
IN_SPLITS = (RET_W, RET_W, RET_W, RET_W, NA_W, NA_W, NA_W, NA_W, GQA_W, GQA_KV_W, GQA_KV_W, GQA_W, N_BRANCH * D_MODEL)
IN_COLS = 4 * RET_W + 4 * NA_W + 2 * GQA_W + 2 * GQA_KV_W + N_BRANCH * D_MODEL

kernel_name = 'hybrid_retention_natten_gqa_prefix_trunk'


def rms_norm(x, w):
    xf = x.astype(jnp.float32)
    y = xf * lax.rsqrt(jnp.mean(xf * xf, axis=-1, keepdims=True) + EPS)
    return (y * w.astype(jnp.float32)).astype(x.dtype)


def heads(a, h):
    return a.reshape(a.shape[:-1] + (h, a.shape[-1] // h))


def split_cols(p):
    bounds = np.cumsum(IN_SPLITS)[:-1].tolist()
    return jnp.split(p, bounds, axis=-1)


def _flip(a, rev):
    return a[:, ::-1] if rev else a


def axial_rope(x):
    n, dh = x.shape[1], x.shape[-1]
    half = dh // 2
    quarter = half // 2
    t = jnp.arange(n)
    pos_r = (t // GRID_W).astype(jnp.float32)
    pos_c = (t % GRID_W).astype(jnp.float32)
    freqs = ROPE_THETA ** (-jnp.arange(quarter, dtype=jnp.float32) / quarter)

    def rot(xa, pos):
        ang = pos[:, None] * freqs[None, :]
        cos = jnp.cos(ang)[None, :, None, :]
        sin = jnp.sin(ang)[None, :, None, :]
        x1, x2 = xa[..., :quarter], xa[..., quarter:]
        return jnp.concatenate([x1 * cos - x2 * sin, x2 * cos + x1 * sin], axis=-1)

    xf = x.astype(jnp.float32)
    return jnp.concatenate([rot(xf[..., :half], pos_r), rot(xf[..., half:], pos_c)], axis=-1).astype(x.dtype)


def attend(q, k, v):
    s = jnp.einsum('btkgd,bskd->bkgts', q, k).astype(jnp.float32) * (q.shape[-1] ** -0.5)
    p = jax.nn.softmax(s, axis=-1)
    return jnp.einsum('bkgts,bskd->btkgd', p, v)


def ret_chunkwise(q, k, v, log_g, s0):
    b, t, h, dk = q.shape
    dv = v.shape[-1]
    c = RET_CHUNK
    n = t // c
    qc = q.reshape(b, n, c, h, dk)
    kc = k.reshape(b, n, c, h, dk)
    vc = v.reshape(b, n, c, h, dv)
    pos = jnp.arange(c, dtype=jnp.float32)
    diff = pos[:, None] - pos[None, :]
    decay = jnp.where(diff >= 0, jnp.exp(jnp.maximum(diff, 0.0)[None] * log_g[:, None, None]), 0.0)
    scores = jnp.einsum('bnihd,bnjhd->bnhij', qc, kc) * decay[None, None]
    intra = jnp.einsum('bnhij,bnjhe->bnihe', scores, vc)
    zeta = jnp.exp((c - 1 - pos)[None, :] * log_g[:, None])
    u = jnp.einsum('bnjhd,hj,bnjhe->nbhde', kc, zeta, vc)
    g_chunk = jnp.exp(c * log_g)[None, :, None, None]

    def step(s, u_n):
        return g_chunk * s + u_n, s

    s_fin, s_prev = lax.scan(step, s0, u)
    xi = jnp.exp((pos + 1.0)[None, :] * log_g[:, None])
    inter = jnp.einsum('bnihd,hi,nbhde->bnihe', qc, xi, s_prev)
    return (intra + inter).reshape(b, t, h, dv), s_fin


def ret_state(k, v, log_g):
    t = k.shape[1]
    pos = jnp.arange(t, dtype=jnp.float32)
    w = jnp.exp((t - 1 - pos)[None, :] * log_g[:, None])
    return jnp.einsum('bthd,ht,bthe->bhde', k, w, v)


def bidir_retention(q, k, v, qc, kc, vc, log_g, need_ctx):
    b, _, h, dk = q.shape
    dv = v.shape[-1]
    o_lat, o_ctx = [], []
    for d in range(2):
        rev = d == 1
        if need_ctx:
            s0 = jnp.zeros((b, h, dk, dv), jnp.float32)
            oc, s_ctx = ret_chunkwise(_flip(qc, rev), _flip(kc, rev), _flip(vc, rev), log_g[d], s0)
            o_ctx.append(_flip(oc, rev))
        else:
            s_ctx = ret_state(_flip(kc, rev), _flip(vc, rev), log_g[d])
        ol, _ = ret_chunkwise(_flip(q, rev), _flip(k, rev), _flip(v, rev), log_g[d], s_ctx)
        o_lat.append(_flip(ol, rev))
    return o_lat[0] + o_lat[1], (o_ctx[0] + o_ctx[1] if need_ctx else None)


def head_group_norm(o, w):
    mu = jnp.mean(o, axis=-1, keepdims=True)
    var = jnp.mean(jnp.square(o - mu), axis=-1, keepdims=True)
    y = (o - mu) * lax.rsqrt(var + EPS)
    return y.reshape(o.shape[0], o.shape[1], -1) * w.astype(jnp.float32)


def neighbourhood_attention(q, k, v, kc, vc, rpb, rows):
    b, n, h, dh = q.shape
    wr = min(NA_WIN_ROWS, rows)
    qcb = NA_WIN_COLS
    ncb = GRID_W // qcb
    band = 2 * qcb
    qcol = np.arange(ncb)[:, None] * qcb + np.arange(qcb)[None, :]
    cstart = np.clip(qcol - NA_WIN_COLS // 2, 0, GRID_W - NA_WIN_COLS)
    bstart = np.clip(np.arange(ncb) * qcb - NA_WIN_COLS // 2, 0, GRID_W - band)
    kcol = bstart[:, None] + np.arange(band)[None, :]
    col_ok = jnp.asarray((kcol[:, None, :] >= cstart[:, :, None]) & (kcol[:, None, :] < cstart[:, :, None] + NA_WIN_COLS))
    dc_idx = np.clip(kcol[:, None, :] - qcol[:, :, None] + NA_WIN_COLS - 1, 0, 2 * NA_WIN_COLS - 2)
    rpb_c = rpb.astype(jnp.float32)[:, :, dc_idx]

    kg = k.reshape(b, rows, GRID_W, h, dh)[:, :, kcol]
    vg = v.reshape(b, rows, GRID_W, h, dh)[:, :, kcol]
    rb = math.gcd(rows, NA_ROW_BLOCK)
    nrb = rows // rb
    qblocks = q.reshape(b, nrb, rb, ncb, qcb, h, dh).transpose(1, 0, 2, 3, 4, 5, 6)
    scale = dh ** -0.5
    n_loc = wr * band

    def block(args):
        i, qb = args
        r = i * rb + jnp.arange(rb)
        rs = jnp.clip(r - wr // 2, 0, rows - wr)
        ridx = rs[:, None] + jnp.arange(wr)[None, :]
        kb = jnp.take(kg, ridx, axis=1)
        vb = jnp.take(vg, ridx, axis=1)
        dr_idx = ridx - r[:, None] + NA_WIN_ROWS - 1
        bias = rpb_c[:, dr_idx].transpose(0, 1, 3, 4, 2, 5)
        s = jnp.einsum('brjqhd,brwjkhd->bhrjqwk', qb, kb).astype(jnp.float32) * scale + bias[None]
        s = jnp.where(col_ok[:, :, None, :], s, -jnp.inf)
        s_ctx = jnp.einsum('brjqhd,blhd->bhrjql', qb, kc).astype(jnp.float32) * scale
        s_all = jnp.concatenate([s.reshape(s.shape[:5] + (n_loc,)), s_ctx], axis=-1)
        p = jax.nn.softmax(s_all, axis=-1)
        p_loc = p[..., :n_loc].reshape(s.shape)
        p_ctx = p[..., n_loc:]
        return jnp.einsum('bhrjqwk,brwjkhd->brjqhd', p_loc, vb) + jnp.einsum('bhrjql,blhd->brjqhd', p_ctx, vc)

    o = lax.map(block, (jnp.arange(nrb), qblocks))
    return o.transpose(1, 0, 2, 3, 4, 5, 6).reshape(b, n, h * dh)


def gqa_blocks(q, k_all, v_all):
    b, n, hq, dh = q.shape
    hk = k_all.shape[2]
    nb = n // Q_BLOCK
    qb = q.reshape(b, nb, Q_BLOCK, hk, hq // hk, dh).transpose(1, 0, 2, 3, 4, 5)
    o = lax.map(lambda qi: attend(qi, k_all, v_all), qb)
    return o.transpose(1, 0, 2, 3, 4, 5).reshape(b, n, hq * dh)


def merge_branches(ret_o, na_o, gqa_o, gate_logits, w_ret_o, w_na_o, w_gqa_o, w_out):
    dt = w_out.dtype
    g_ret, g_na, g_gqa = jnp.split(jax.nn.sigmoid(gate_logits.astype(jnp.float32)), N_BRANCH, axis=-1)
    m = (g_ret * (ret_o.astype(dt) @ w_ret_o) + g_na * (na_o.astype(dt) @ w_na_o)
         + g_gqa * (gqa_o.astype(dt) @ w_gqa_o))
    return m.astype(dt) @ w_out


def hybrid_layer(x, y, c_silu, cc_silu, ada_w, ada_b, norm_w, w_in, ret_log_decay, ret_gn_w, na_rpb,
                 q_norm_w, k_norm_w, w_ret_o, w_na_o, w_gqa_o, w_out, need_ctx):
    b, n, _ = x.shape
    l = y.shape[1]
    rows = n // GRID_W
    g_grp = GQA_HEADS // GQA_KV_HEADS
    shift_x, scale_x, gate_x = jnp.split((c_silu @ ada_w + ada_b)[:, None, :], 3, axis=-1)
    shift_y, scale_y, gate_y = jnp.split(cc_silu @ ada_w + ada_b, 3, axis=-1)
    hx = rms_norm(x, norm_w) * (1.0 + scale_x) + shift_x
    hy = rms_norm(y, norm_w) * (1.0 + scale_y) + shift_y
    rq, rk, rv, rg, nq, nk, nv, ng, gq, gk, gv, gg, mg = split_cols(hx @ w_in)
    crq, crk, crv, crg, cnq, cnk, cnv, cng, cgq, cgk, cgv, cgg, cmg = split_cols(hy @ w_in)

    log_g = -jnp.exp(ret_log_decay.astype(jnp.float32))
    ksc = RET_DK ** -0.5
    o_lat, o_ctx = bidir_retention(heads(rq, RET_HEADS), heads(rk, RET_HEADS) * ksc, heads(rv, RET_HEADS),
                                   heads(crq, RET_HEADS), heads(crk, RET_HEADS) * ksc, heads(crv, RET_HEADS),
                                   log_g, need_ctx)
    ret_x = head_group_norm(o_lat, ret_gn_w) * jax.nn.silu(rg)

    kc_na = heads(cnk, NA_HEADS)
    vc_na = heads(cnv, NA_HEADS)
    na_x = neighbourhood_attention(heads(nq, NA_HEADS), heads(nk, NA_HEADS), heads(nv, NA_HEADS),
                                   kc_na, vc_na, na_rpb, rows) * jax.nn.silu(ng)

    kc_g = rms_norm(heads(cgk, GQA_KV_HEADS), k_norm_w)
    vc_g = heads(cgv, GQA_KV_HEADS)
    q_g = axial_rope(rms_norm(heads(gq, GQA_HEADS), q_norm_w))
    k_g = axial_rope(rms_norm(heads(gk, GQA_KV_HEADS), k_norm_w))
    k_all = jnp.concatenate([k_g, kc_g], axis=1)
    v_all = jnp.concatenate([heads(gv, GQA_KV_HEADS), vc_g], axis=1)
    gqa_x = gqa_blocks(q_g, k_all, v_all) * jax.nn.silu(gg)

    x = x + gate_x * merge_branches(ret_x, na_x, gqa_x, mg, w_ret_o, w_na_o, w_gqa_o, w_out)

    if need_ctx:
        ret_y = head_group_norm(o_ctx, ret_gn_w) * jax.nn.silu(crg)
        na_y = attend(heads(cnq, NA_HEADS)[:, :, :, None, :], kc_na, vc_na).reshape(b, l, NA_W) * jax.nn.silu(cng)
        qc_g = rms_norm(heads(cgq, GQA_HEADS), q_norm_w).reshape(b, l, GQA_KV_HEADS, g_grp, GQA_DH)
        gqa_y = attend(qc_g, kc_g, vc_g).reshape(b, l, GQA_W) * jax.nn.silu(cgg)
        y = y + gate_y * merge_branches(ret_y, na_y, gqa_y, cmg, w_ret_o, w_na_o, w_gqa_o, w_out)
    return x, y


def setup_inputs(seed: int = 0) -> dict:
    key = jax.random.key(seed)
    ks = jax.random.split(key, 18)
    f32 = jnp.float32
    d = D_MODEL

    def nrm(k, shape, scale):
        return jax.random.normal(k, shape, f32) * scale

    decay_base = jnp.asarray(np.log(-np.log(1.0 - 2.0 ** (-5.0 - np.arange(RET_HEADS)))), f32)
    return {
        'x': nrm(ks[0], (BATCH, SEQ, d), 1.0),
        'c': nrm(ks[1], (BATCH, d), 1.0),
        'ctx': nrm(ks[2], (BATCH, CTX_LEN, d), 1.0),
        'c_ctx': nrm(ks[3], (d,), 1.0),
        'ada_w': nrm(ks[4], (DEPTH, d, 3 * d), d ** -0.5),
        'ada_b': nrm(ks[5], (DEPTH, 3 * d), 0.02),
        'norm_w': 1.0 + nrm(ks[6], (DEPTH, d), 0.02),
        'w_in': nrm(ks[7], (DEPTH, d, IN_COLS), d ** -0.5),
        'ret_log_decay': decay_base + nrm(ks[8], (DEPTH, 2, RET_HEADS), 0.1),
        'ret_gn_w': 1.0 + nrm(ks[9], (DEPTH, RET_W), 0.02),
        'na_rpb': nrm(ks[10], (DEPTH, NA_HEADS, 2 * NA_WIN_ROWS - 1, 2 * NA_WIN_COLS - 1), 0.05),
        'q_norm_w': 1.0 + nrm(ks[11], (DEPTH, GQA_DH), 0.02),
        'k_norm_w': 1.0 + nrm(ks[12], (DEPTH, GQA_DH), 0.02),
        'w_ret_o': nrm(ks[13], (DEPTH, RET_W, d), RET_W ** -0.5),
        'w_na_o': nrm(ks[14], (DEPTH, NA_W, d), NA_W ** -0.5),
        'w_gqa_o': nrm(ks[15], (DEPTH, GQA_W, d), GQA_W ** -0.5),
        'w_out': nrm(ks[16], (DEPTH, d, d), d ** -0.5),
        'final_norm_w': 1.0 + nrm(ks[17], (d,), 0.02),
    }


def reference(x, c, ctx, c_ctx, ada_w, ada_b, norm_w, w_in, ret_log_decay, ret_gn_w, na_rpb,
              q_norm_w, k_norm_w, w_ret_o, w_na_o, w_gqa_o, w_out, final_norm_w):
    c_silu = jax.nn.silu(c)
    cc_silu = jax.nn.silu(c_ctx)
    y = ctx
    for layer in range(DEPTH):
        x, y = hybrid_layer(x, y, c_silu, cc_silu, ada_w[layer], ada_b[layer], norm_w[layer], w_in[layer],
                            ret_log_decay[layer], ret_gn_w[layer], na_rpb[layer], q_norm_w[layer], k_norm_w[layer],
                            w_ret_o[layer], w_na_o[layer], w_gqa_o[layer], w_out[layer],
                            need_ctx=layer < DEPTH - 1)
    return rms_norm(x, final_norm_w)
```

```python
import functools

import numpy as np
import jax
import jax.numpy as jnp
from jax import lax
from jax.experimental import pallas as pl
from jax.experimental.pallas import tpu as pltpu

F32 = jnp.float32
BF16 = jnp.bfloat16

D_MODEL = 1024
GRID_W = 64
EPS = 1e-6
ROPE_THETA = 10000.0
RET_HEADS = 4
RET_D = 128
RET_W = RET_HEADS * RET_D
RET_CHUNK = 128
NA_HEADS = 8
HEAD_D = 64
NA_W = NA_HEADS * HEAD_D
NA_WIN_ROWS = 8
NA_WIN_COLS = 16
GQA_HEADS = 8
GQA_KV_HEADS = 2
GQA_W = GQA_HEADS * HEAD_D
GQA_KV_W = GQA_KV_HEADS * HEAD_D
N_BRANCH = 3
LANES = 128
ROW_BLOCK = 256
NA_UNIT_ROWS = ROW_BLOCK // GRID_W
NA_WIN_UNITS = 3
MASKED = -1e30
VMEM_LIMIT = 56 * 1024 * 1024

_SPLITS = (RET_W, RET_W, RET_W, RET_W, NA_W, NA_W, NA_W, NA_W, GQA_W, GQA_KV_W, GQA_KV_W, GQA_W,
           N_BRANCH * D_MODEL)
_OFFS = np.concatenate([[0], np.cumsum(_SPLITS)])
IN_COLS = int(_OFFS[-1])
(O_RQ, O_RK, O_RV, O_RG, O_NQ, O_NK, O_NV, O_NG, O_GQ, O_GK, O_GV, O_GG, O_MG) = (int(o) for o in _OFFS[:-1])


def _params(sem, vmem=VMEM_LIMIT):
    return pltpu.CompilerParams(dimension_semantics=sem, vmem_limit_bytes=vmem)


def _silu(x):
    return x / (1.0 + jnp.exp(-x))


def _sigmoid(x):
    return 1.0 / (1.0 + jnp.exp(-x))


def _lane_dims():
    l = np.arange(LANES)
    half = l // 64
    slot = (l % 64) // 32
    r = l % 32
    d = np.where(r < 16, r, 32 + r - 16) + 16 * half
    return slot, d


def _gqa_perms():
    slot, d = _lane_dims()
    q_cols = np.concatenate([np.where(slot == 0, j, GQA_HEADS // 2 + j) * HEAD_D + d
                             for j in range(GQA_HEADS // 2)])
    k_cols = slot * HEAD_D + d
    l = np.arange(LANES)
    o_cols = np.concatenate([np.where(l < 64, j, GQA_HEADS // 2 + j) * HEAD_D + (l % 64)
                             for j in range(GQA_HEADS // 2)])
    return q_cols, k_cols, o_cols


def _rope_tables(n, t_total):
    _, d = _lane_dims()
    quarter = HEAD_D // 4
    freqs = ROPE_THETA ** (-jnp.arange(quarter, dtype=F32) / quarter)
    t = jnp.arange(n)
    pos_r = (t // GRID_W).astype(F32)
    pos_c = (t % GRID_W).astype(F32)
    dd = d % 32
    fi = dd % quarter
    use_col = (d // 32) == 1
    first = dd < quarter
    ang = jnp.where(use_col[None, :], pos_c[:, None], pos_r[:, None]) * freqs[fi][None, :]
    cos = jnp.cos(ang)
    sin = jnp.where(first[None, :], -jnp.sin(ang), jnp.sin(ang))
    pad = t_total - n
    cos = jnp.concatenate([cos, jnp.ones((pad, LANES), F32)], axis=0)
    sin = jnp.concatenate([sin, jnp.zeros((pad, LANES), F32)], axis=0)
    return cos, sin


def _ada_kernel(c_ref, w_ref, b_ref, o_ref):
    cs = _silu(c_ref[...])
    o_ref[0] = jnp.dot(cs, w_ref[0], preferred_element_type=F32) + b_ref[0]


def _ada_call(c_rows, ada_w, ada_b):
    depth = ada_w.shape[0]
    rows = c_rows.shape[0]
    nb = 3
    return pl.pallas_call(
        _ada_kernel,
        grid=(depth, nb),
        in_specs=[
            pl.BlockSpec((rows, D_MODEL), lambda l, j: (0, 0)),
            pl.BlockSpec((1, D_MODEL, D_MODEL), lambda l, j: (l, 0, j)),
            pl.BlockSpec((1, 1, D_MODEL), lambda l, j: (l, 0, j)),
        ],
        out_specs=pl.BlockSpec((1, rows, D_MODEL), lambda l, j: (l, 0, j)),
        out_shape=jax.ShapeDtypeStruct((depth, rows, nb * D_MODEL), F32),
        compiler_params=_params(("parallel", "parallel")),
        name="ada_mod",
    )(c_rows, ada_w, ada_b.reshape(depth, 1, nb * D_MODEL))


def _head_mean_sq(t, mask_a):
    sq = t * t
    s_all = jnp.sum(sq, axis=-1, keepdims=True)
    s_a = jnp.sum(jnp.where(mask_a, sq, 0.0), axis=-1, keepdims=True)
    return jnp.where(mask_a, s_a, s_all - s_a) * (1.0 / HEAD_D)


def _norm_rope(t, w_lane, cos, sin, mask_a):
    tn = t * lax.rsqrt(_head_mean_sq(t, mask_a) + EPS) * w_lane
    return tn * cos + pltpu.roll(tn, LANES // 2, 1) * sin


def _inproj_kernel(x_ref, mod_ref, nw_ref, w_ref, cos_ref, sin_ref, qw_ref, kw_ref,
                   rq_ref, rk_ref, rv_ref, rg_ref, nq_ref, nk_ref, nv_ref, ng_ref,
                   gq_ref, gk_ref, gv_ref, gg_ref, mg_ref):
    x = x_ref[0]
    ms = jnp.mean(x * x, axis=-1, keepdims=True)
    y = x * lax.rsqrt(ms + EPS) * nw_ref[...]
    mod = mod_ref[0]
    shift = mod[:, :D_MODEL]
    scale = mod[:, D_MODEL:2 * D_MODEL]
    h = (y * (1.0 + scale) + shift).astype(BF16)

    def proj(off, width):
        return jnp.dot(h, w_ref[:, off:off + width], preferred_element_type=F32)

    qk_scale = HEAD_D ** -0.5
    rq_ref[0] = proj(O_RQ, RET_W).astype(BF16)
    rk_ref[0] = proj(O_RK, RET_W).astype(BF16)
    rv_ref[0] = proj(O_RV, RET_W).astype(BF16)
    rg_ref[0] = _silu(proj(O_RG, RET_W)).astype(BF16)
    nq_ref[0] = (proj(O_NQ, NA_W) * qk_scale).astype(BF16)
    nk_ref[0] = proj(O_NK, NA_W).astype(BF16)
    nv_ref[0] = proj(O_NV, NA_W).astype(BF16)
    ng_ref[0] = _silu(proj(O_NG, NA_W)).astype(BF16)

    lane = lax.broadcasted_iota(jnp.int32, (1, LANES), 1)
    mask_a = (lane % 64) < 32
    cos = cos_ref[...]
    sin = sin_ref[...]
    qw = qw_ref[...]
    for j in range(GQA_W // LANES):
        t = proj(O_GQ + j * LANES, LANES)
        gq_ref[0, :, j * LANES:(j + 1) * LANES] = (_norm_rope(t, qw, cos, sin, mask_a) * qk_scale).astype(BF16)
    gk_ref[0] = _norm_rope(proj(O_GK, GQA_KV_W), kw_ref[...], cos, sin, mask_a).astype(BF16)
    gv_ref[0] = proj(O_GV, GQA_KV_W).astype(BF16)
    gg_ref[0] = _silu(proj(O_GG, GQA_W)).astype(BF16)
    for j in range(N_BRANCH):
        mg_ref[0, :, j * D_MODEL:(j + 1) * D_MODEL] = _sigmoid(proj(O_MG + j * D_MODEL, D_MODEL)).astype(BF16)


def _mod_index(n_lat_blocks, batch):
    return lambda b, i: (jnp.where(i < n_lat_blocks, b, batch), 0, 0)


def _inproj_call(xa, mod, norm_w, w_in, cos, sin, qw, kw, n_lat):
    b, t, _ = xa.shape
    tm = ROW_BLOCK
    widths = (RET_W, RET_W, RET_W, RET_W, NA_W, NA_W, NA_W, NA_W, GQA_W, GQA_KV_W, GQA_KV_W, GQA_W,
              N_BRANCH * D_MODEL)
    tok = lambda w: pl.BlockSpec((1, tm, w), lambda bi, i: (bi, i, 0))
    const2 = lambda shape: pl.BlockSpec(shape, lambda bi, i: (0, 0))
    return pl.pallas_call(
        _inproj_kernel,
        grid=(b, t // tm),
        in_specs=[
            tok(D_MODEL),
            pl.BlockSpec((1, 1, 3 * D_MODEL), _mod_index(n_lat // tm, b)),
            const2((1, D_MODEL)),
            pl.BlockSpec((D_MODEL, IN_COLS), lambda bi, i: (0, 0), pipeline_mode=pl.Buffered(1)),
            pl.BlockSpec((tm, LANES), lambda bi, i: (i, 0)),
            pl.BlockSpec((tm, LANES), lambda bi, i: (i, 0)),
            const2((1, LANES)),
            const2((1, LANES)),
        ],
        out_specs=[tok(w) for w in widths],
        out_shape=[jax.ShapeDtypeStruct((b, t, w), BF16) for w in widths],
        compiler_params=_params(("parallel", "parallel")),
        name="in_proj",
    )(xa, mod, norm_w, w_in, cos, sin, qw, kw)


def _ret_kernel(rld_ref, q_ref, k_ref, v_ref, o_ref, state, dec, xi, zeta, gch):
    d = pl.program_id(1)
    s = pl.program_id(2)
    c = RET_CHUNK
    ksc = RET_D ** -0.5

    @pl.when(s == 0)
    def _():
        state[...] = jnp.zeros_like(state)
        ii = lax.broadcasted_iota(jnp.int32, (c, c), 0)
        jj = lax.broadcasted_iota(jnp.int32, (c, c), 1)
        delta = jnp.where(d == 0, ii - jj, jj - ii).astype(F32)
        ip = jnp.where(d == 0, ii, c - 1 - ii).astype(F32)
        for h in range(RET_HEADS):
            lg = -jnp.exp(jnp.full((c, c), rld_ref[d, h], F32))
            dec[h] = jnp.where(delta >= 0, jnp.exp(jnp.maximum(delta, 0.0) * lg), 0.0) * ksc
            xi[h] = jnp.exp((ip + 1.0) * lg)
            zeta[h] = jnp.exp((c - 1.0 - ip) * lg) * ksc
            gch[h] = jnp.exp(float(c) * lg)

    for h in range(RET_HEADS):
        sl = slice(h * RET_D, (h + 1) * RET_D)
        q = q_ref[0, :, sl]
        k = k_ref[0, :, sl]
        v = v_ref[0, :, sl]
        scores = lax.dot_general(q, k, (((1,), (1,)), ((), ())), preferred_element_type=F32)
        intra = jnp.dot((scores * dec[h]).astype(BF16), v, preferred_element_type=F32)
        st = state[h]
        inter = jnp.dot(q, st.astype(BF16), preferred_element_type=F32) * xi[h]
        o_ref[0, 0, :, sl] = intra + inter
        vz = (v.astype(F32) * zeta[h]).astype(BF16)
        upd = lax.dot_general(k, vz, (((0,), (0,)), ((), ())), preferred_element_type=F32)
        state[h] = gch[h] * st + upd


def _ret_call(rld, rq, rk, rv, n_lat):
    b, t, _ = rq.shape
    c = RET_CHUNK
    nc = t // c
    ncn = n_lat // c

    def pos(d, s):
        return jnp.where(d == 0, (s + ncn) % nc, nc - 1 - s)

    tok = pl.BlockSpec((1, c, RET_W), lambda bi, d, s: (bi, pos(d, s), 0))
    sq = pltpu.VMEM((RET_HEADS, c, c), F32)
    return pl.pallas_call(
        _ret_kernel,
        grid=(b, 2, nc),
        in_specs=[pl.BlockSpec(memory_space=pltpu.SMEM), tok, tok, tok],
        out_specs=pl.BlockSpec((1, 1, c, RET_W), lambda bi, d, s: (d, bi, pos(d, s), 0)),
        out_shape=jax.ShapeDtypeStruct((2, b, t, RET_W), F32),
        scratch_shapes=[sq, sq, sq, sq, sq],
        compiler_params=_params(("parallel", "arbitrary", "arbitrary")),
        name="retention",
    )(rld, rq, rk, rv)


def _pair_split(q, mask_a):
    zero = jnp.zeros_like(q)
    return jnp.concatenate([jnp.where(mask_a, q, zero), jnp.where(mask_a, zero, q)], axis=0)


def _dot_t(a, b):
    return lax.dot_general(a, b, (((1,), (1,)), ((), ())), preferred_element_type=F32)


def _na_kernel(q_ref, k0_ref, k1_ref, k2_ref, v0_ref, v1_ref, v2_ref, kc_ref, vc_ref, bias_ref, g_ref, o_ref):
    m = q_ref.shape[1]
    lane = lax.broadcasted_iota(jnp.int32, (1, LANES), 1)
    lo = lane < 64
    q2 = _pair_split(q_ref[0], lo)
    bias = jnp.concatenate([bias_ref[0, 0, 0], bias_ref[0, 0, 1]], axis=0)
    u = ROW_BLOCK
    s_loc = [_dot_t(q2, kr[0]) + bias[:, i * u:(i + 1) * u] for i, kr in enumerate((k0_ref, k1_ref, k2_ref))]
    s_ctx = _dot_t(q2, kc_ref[0])
    mx = jnp.max(s_ctx, axis=-1, keepdims=True)
    for sl in s_loc:
        mx = jnp.maximum(mx, jnp.max(sl, axis=-1, keepdims=True))
    p_ctx = jnp.exp(s_ctx - mx)
    den = jnp.sum(p_ctx, axis=-1, keepdims=True)
    acc = jnp.dot(p_ctx.astype(BF16), vc_ref[0], preferred_element_type=F32)
    for sl, vr in zip(s_loc, (v0_ref, v1_ref, v2_ref)):
        p = jnp.exp(sl - mx)
        den = den + jnp.sum(p, axis=-1, keepdims=True)
        acc = acc + jnp.dot(p.astype(BF16), vr[0], preferred_element_type=F32)
    o2 = acc / den
    o = jnp.where(lo, o2[:m], o2[m:])
    o_ref[0] = (o * g_ref[0].astype(F32)).astype(BF16)


def _na_bias_tables(rpb, rows):
    ur = NA_UNIT_ROWS
    g_units = rows // ur
    rho = np.arange(ROW_BLOCK) // GRID_W
    qc = np.arange(ROW_BLOCK) % GRID_W
    kappa = np.arange(NA_WIN_UNITS * ROW_BLOCK) // GRID_W
    kc = np.arange(NA_WIN_UNITS * ROW_BLOCK) % GRID_W
    cstart = np.clip(qc - NA_WIN_COLS // 2, 0, GRID_W - NA_WIN_COLS)
    col_ok = (kc[None, :] >= cstart[:, None]) & (kc[None, :] < cstart[:, None] + NA_WIN_COLS)
    dc = np.clip(kc[None, :] - qc[:, None] + NA_WIN_COLS - 1, 0, 2 * NA_WIN_COLS - 2)
    tables = []
    for g in (0, 1, g_units - 1):
        r = g * ur + rho
        ws = int(np.clip(g - 1, 0, g_units - NA_WIN_UNITS)) * ur
        kr = ws + kappa
        rs = np.clip(r - NA_WIN_ROWS // 2, 0, rows - NA_WIN_ROWS)
        row_ok = (kr[None, :] >= rs[:, None]) & (kr[None, :] < rs[:, None] + NA_WIN_ROWS)
        dr = np.clip(kr[None, :] - r[:, None] + NA_WIN_ROWS - 1, 0, 2 * NA_WIN_ROWS - 2)
        ok = jnp.asarray(row_ok & col_ok)
        tables.append(jnp.where(ok[None], rpb.astype(F32)[:, dr, dc], MASKED))
    tables.append(jnp.full_like(tables[0], MASKED))
    tb = jnp.stack(tables)
    return tb.reshape(4, NA_HEADS // 2, 2, ROW_BLOCK, NA_WIN_UNITS * ROW_BLOCK)


def _na_call(nq, nk, nv, ng, bias, n_lat, n_ctx):
    b, t, _ = nq.shape
    u = ROW_BLOCK
    g_units = n_lat // u
    n_units = t // u
    pairs = NA_W // LANES

    def win(i):
        return lambda bi, j, g: (bi, jnp.clip(g - 1, 0, g_units - NA_WIN_UNITS) + i, j)

    def case(bi, j, g):
        cs = jnp.where(g == 0, 0, jnp.where(g < g_units - 1, 1, jnp.where(g == g_units - 1, 2, 3)))
        return (cs, j, 0, 0, 0)

    tok = pl.BlockSpec((1, u, LANES), lambda bi, j, g: (bi, g, j))
    ctx = pl.BlockSpec((1, n_ctx, LANES), lambda bi, j, g: (bi, n_lat // n_ctx, j))
    kv = [pl.BlockSpec((1, u, LANES), win(i)) for i in range(NA_WIN_UNITS)]
    return pl.pallas_call(
        _na_kernel,
        grid=(b, pairs, n_units),
        in_specs=[tok] + kv + kv + [ctx, ctx,
                                    pl.BlockSpec((1, 1, 2, u, NA_WIN_UNITS * u), case), tok],
        out_specs=tok,
        out_shape=jax.ShapeDtypeStruct((b, t, NA_W), BF16),
        compiler_params=_params(("parallel", "parallel", "arbitrary")),
        name="nbr_attn",
    )(nq, nk, nk, nk, nv, nv, nv, nk, nv, bias, ng)


def _gqa_kernel(q_ref, k_ref, v_ref, g_ref, o_ref, m_sc, l_sc, acc_sc, *, n_lat, tk):
    i = pl.program_id(1)
    tq = q_ref.shape[1]
    n_chunks = k_ref.shape[1] // tk
    start = jnp.where(i * tq >= n_lat, n_lat // tk, 0)
    lane = lax.broadcasted_iota(jnp.int32, (1, LANES), 1)
    mask_a = (lane % 64) < 32
    lo = lane < 64
    for j in range(GQA_W // LANES):
        sl = slice(j * LANES, (j + 1) * LANES)
        q2 = _pair_split(q_ref[0, :, sl], mask_a)
        m_sc[...] = jnp.full_like(m_sc, MASKED)
        l_sc[...] = jnp.zeros_like(l_sc)
        acc_sc[...] = jnp.zeros_like(acc_sc)

        def body(c, carry):
            off = pl.multiple_of(c * tk, tk)
            s = _dot_t(q2, k_ref[0, pl.ds(off, tk), :])
            m_old = m_sc[...]
            m_new = jnp.maximum(m_old, jnp.max(s, axis=-1, keepdims=True))
            alpha = jnp.exp(m_old - m_new)
            p = jnp.exp(s - m_new)
            l_sc[...] = alpha * l_sc[...] + jnp.sum(p, axis=-1, keepdims=True)
            acc_sc[...] = alpha * acc_sc[...] + jnp.dot(p.astype(BF16), v_ref[0, pl.ds(off, tk), :],
                                                        preferred_element_type=F32)
            m_sc[...] = m_new
            return carry

        lax.fori_loop(start, n_chunks, body, 0)
        o2 = acc_sc[...] / l_sc[...]
        o = jnp.where(lo, o2[:tq], o2[tq:])
        o_ref[0, :, sl] = (o * g_ref[0, :, sl].astype(F32)).astype(BF16)


def _gqa_call(gq, gk, gv, gg, n_lat):
    b, t, _ = gq.shape
    tq = ROW_BLOCK
    tk = ROW_BLOCK
    tok = pl.BlockSpec((1, tq, GQA_W), lambda bi, i: (bi, i, 0))
    kv = pl.BlockSpec((1, t, GQA_KV_W), lambda bi, i: (bi, 0, 0))
    return pl.pallas_call(
        functools.partial(_gqa_kernel, n_lat=n_lat, tk=tk),
        grid=(b, t // tq),
        in_specs=[tok, kv, kv, tok],
        out_specs=tok,
        out_shape=jax.ShapeDtypeStruct((b, t, GQA_W), BF16),
        scratch_shapes=[pltpu.VMEM((2 * tq, 1), F32), pltpu.VMEM((2 * tq, 1), F32),
                        pltpu.VMEM((2 * tq, LANES), F32)],
        compiler_params=_params(("parallel", "arbitrary")),
        name="gqa_attn",
    )(gq, gk, gv, gg)


def _merge_kernel(x_ref, mod_ref, of_ref, ob_ref, rg_ref, gnw_ref, na_ref, gqa_ref, mg_ref,
                  wr_ref, wn_ref, wg_ref, wo_ref, fw_ref, o_ref, *, final):
    o = of_ref[0, 0] + ob_ref[0, 0]
    parts = []
    for h in range(RET_HEADS):
        oh = o[:, h * RET_D:(h + 1) * RET_D]
        mu = jnp.mean(oh, axis=-1, keepdims=True)
        dev = oh - mu
        var = jnp.mean(dev * dev, axis=-1, keepdims=True)
        parts.append(dev * lax.rsqrt(var + EPS))
    ret_x = (jnp.concatenate(parts, axis=-1) * gnw_ref[...] * rg_ref[0].astype(F32)).astype(BF16)
    m = mg_ref[0, :, :D_MODEL].astype(F32) * jnp.dot(ret_x, wr_ref[...], preferred_element_type=F32)
    m = m + mg_ref[0, :, D_MODEL:2 * D_MODEL].astype(F32) * jnp.dot(na_ref[0], wn_ref[...],
                                                                    preferred_element_type=F32)
    m = m + mg_ref[0, :, 2 * D_MODEL:].astype(F32) * jnp.dot(gqa_ref[0], wg_ref[...],
                                                             preferred_element_type=F32)
    out = jnp.dot(m.astype(BF16), wo_ref[...], preferred_element_type=F32)
    gate = mod_ref[0][:, 2 * D_MODEL:]
    xn = x_ref[0] + gate * out
    if final:
        ms = jnp.mean(xn * xn, axis=-1, keepdims=True)
        xn = xn * lax.rsqrt(ms + EPS) * fw_ref[...]
    o_ref[0] = xn


def _merge_call(xa, mod, ret_o, rg, gnw, na_o, gqa_o, mg, wr, wn, wg, wo, fw, n_lat, final):
    b, t, _ = xa.shape
    tm = ROW_BLOCK
    t_out = n_lat if final else t
    tok = lambda w: pl.BlockSpec((1, tm, w), lambda bi, i: (bi, i, 0))
    const2 = lambda shape: pl.BlockSpec(shape, lambda bi, i: (0, 0))
    return pl.pallas_call(
        functools.partial(_merge_kernel, final=final),
        grid=(b, t_out // tm),
        in_specs=[
            tok(D_MODEL),
            pl.BlockSpec((1, 1, 3 * D_MODEL), _mod_index(n_lat // tm, b)),
            pl.BlockSpec((1, 1, tm, RET_W), lambda bi, i: (0, bi, i, 0)),
            pl.BlockSpec((1, 1, tm, RET_W), lambda bi, i: (1, bi, i, 0)),
            tok(RET_W),
            const2((1, RET_W)),
            tok(NA_W),
            tok(GQA_W),
            tok(N_BRANCH * D_MODEL),
            const2((RET_W, D_MODEL)),
            const2((NA_W, D_MODEL)),
            const2((GQA_W, D_MODEL)),
            const2((D_MODEL, D_MODEL)),
            const2((1, D_MODEL)),
        ],
        out_specs=tok(D_MODEL),
        out_shape=jax.ShapeDtypeStruct((b, t_out, D_MODEL), F32),
        compiler_params=_params(("parallel", "parallel")),
        name="merge",
    )(xa, mod, ret_o, ret_o, rg, gnw, na_o, gqa_o, mg, wr, wn, wg, wo, fw)


def _forward(x, c, ctx, c_ctx, ada_w, ada_b, norm_w, w_in, ret_log_decay, ret_gn_w, na_rpb,
             q_norm_w, k_norm_w, w_ret_o, w_na_o, w_gqa_o, w_out, final_norm_w):
    b, n, _ = x.shape
    n_ctx = ctx.shape[1]
    depth = ada_w.shape[0]
    t = n + n_ctx
    rows = n // GRID_W
    assert n % ROW_BLOCK == 0 and n_ctx == ROW_BLOCK and n % n_ctx == 0
    assert rows >= NA_WIN_UNITS * NA_UNIT_ROWS and rows % NA_UNIT_ROWS == 0

    q_cols, k_cols, o_cols = _gqa_perms()
    in_perm = np.arange(IN_COLS)
    in_perm[O_GQ:O_GQ + GQA_W] = O_GQ + q_cols
    in_perm[O_GK:O_GK + GQA_KV_W] = O_GK + k_cols
    in_perm[O_GG:O_GG + GQA_W] = O_GG + o_cols
    _, lane_d = _lane_dims()

    cos, sin = _rope_tables(n, t)
    mod_rows = 8 * ((b + 1 + 7) // 8)
    c_rows = jnp.concatenate([c, c_ctx[None, :], jnp.zeros((mod_rows - b - 1, D_MODEL), F32)], axis=0)
    mod_all = _ada_call(c_rows, ada_w, ada_b)

    xa = jnp.concatenate([x, ctx], axis=1)
    for layer in range(depth):
        final = layer == depth - 1
        mod = mod_all[layer].reshape(mod_rows, 1, 3 * D_MODEL)
        w_l = w_in[layer][:, in_perm].astype(BF16)
        qw = q_norm_w[layer][lane_d].reshape(1, LANES)
        kw = k_norm_w[layer][lane_d].reshape(1, LANES)
        (rq, rk, rv, rg, nq, nk, nv, ng, gq, gk, gv, gg, mg) = _inproj_call(
            xa, mod, norm_w[layer].reshape(1, D_MODEL), w_l, cos, sin, qw, kw, n)
        ret_o = _ret_call(ret_log_decay[layer], rq, rk, rv, n)
        na_o = _na_call(nq, nk, nv, ng, _na_bias_tables(na_rpb[layer], rows), n, n_ctx)
        gqa_o = _gqa_call(gq, gk, gv, gg, n)
        xa = _merge_call(xa, mod, ret_o, rg, ret_gn_w[layer].reshape(1, RET_W), na_o, gqa_o, mg,
                         w_ret_o[layer].astype(BF16), w_na_o[layer].astype(BF16),
                         w_gqa_o[layer][o_cols].astype(BF16), w_out[layer].astype(BF16),
                         final_norm_w.reshape(1, D_MODEL), n, final)
    return xa


def kernel(x, c, ctx, c_ctx, ada_w, ada_b, norm_w, w_in, ret_log_decay, ret_gn_w, na_rpb, q_norm_w, k_norm_w,
           w_ret_o, w_na_o, w_gqa_o, w_out, final_norm_w):
    return _forward(x, c, ctx, c_ctx, ada_w, ada_b, norm_w, w_in, ret_log_decay, ret_gn_w, na_rpb,
                    q_norm_w, k_norm_w, w_ret_o, w_na_o, w_gqa_o, w_out, final_norm_w)
```

```python
import functools

import numpy as np
import jax
import jax.numpy as jnp
from jax import lax
from jax.experimental import pallas as pl
from jax.experimental.pallas import tpu as pltpu

F32 = jnp.float32
BF16 = jnp.bfloat16

D_MODEL = 1024
GRID_W = 64
EPS = 1e-6
ROPE_THETA = 10000.0
RET_HEADS = 4
RET_D = 128
RET_W = RET_HEADS * RET_D
RET_CHUNK = 128
NA_HEADS = 8
HEAD_D = 64
NA_W = NA_HEADS * HEAD_D
NA_WIN_ROWS = 8
NA_WIN_COLS = 16
GQA_HEADS = 8
GQA_KV_HEADS = 2
GQA_W = GQA_HEADS * HEAD_D
GQA_KV_W = GQA_KV_HEADS * HEAD_D
N_BRANCH = 3
LANES = 128
ROW_BLOCK = 256
NA_UNIT_ROWS = ROW_BLOCK // GRID_W
NA_WIN_UNITS = 3
GQA_KEY_CHUNK = 1024
LOG2E = 1.4426950408889634
MASKED = -1e30
VMEM_LIMIT = 56 * 1024 * 1024

_SPLITS = (RET_W, RET_W, RET_W, RET_W, NA_W, NA_W, NA_W, NA_W, GQA_W, GQA_KV_W, GQA_KV_W, GQA_W,
           N_BRANCH * D_MODEL)
_OFFS = np.concatenate([[0], np.cumsum(_SPLITS)])
IN_COLS = int(_OFFS[-1])
(O_RQ, O_RK, O_RV, O_RG, O_NQ, O_NK, O_NV, O_NG, O_GQ, O_GK, O_GV, O_GG, O_MG) = (int(o) for o in _OFFS[:-1])


def _params(sem, vmem=VMEM_LIMIT):
    return pltpu.CompilerParams(dimension_semantics=sem, vmem_limit_bytes=vmem)


def _silu(x):
    return x / (1.0 + jnp.exp(-x))


def _sigmoid(x):
    return 1.0 / (1.0 + jnp.exp(-x))


def _lane_dims():
    l = np.arange(LANES)
    half = l // 64
    slot = (l % 64) // 32
    r = l % 32
    d = np.where(r < 16, r, 32 + r - 16) + 16 * half
    return slot, d


def _transpose_cols(w, shape, order, axis=-1):
    axis = axis % w.ndim
    k = len(shape)
    full = w.shape[:axis] + tuple(shape) + w.shape[axis + 1:]
    perm = tuple(range(axis)) + tuple(axis + o for o in order) + tuple(range(axis + k, len(full)))
    return jnp.transpose(w.reshape(full), perm).reshape(w.shape)


def _q_layout(w):
    return _transpose_cols(w, (2, GQA_HEADS // 2, 2, 2, 16), (1, 3, 0, 2, 4))


def _k_layout(w):
    return _transpose_cols(w, (2, 2, 2, 16), (2, 0, 1, 3))


def _o_layout(w, axis=-1):
    return _transpose_cols(w, (2, GQA_HEADS // 2, HEAD_D), (1, 0, 2), axis=axis)


def _permute_w_in(w_in):
    parts = [w_in[..., :O_GQ], _q_layout(w_in[..., O_GQ:O_GK]), _k_layout(w_in[..., O_GK:O_GV]),
             w_in[..., O_GV:O_GG], _o_layout(w_in[..., O_GG:O_MG]), w_in[..., O_MG:]]
    return jnp.concatenate(parts, axis=-1)


def _rope_tables(n, t_total):
    _, d = _lane_dims()
    quarter = HEAD_D // 4
    freqs = ROPE_THETA ** (-jnp.arange(quarter, dtype=F32) / quarter)
    t = jnp.arange(n)
    pos_r = (t // GRID_W).astype(F32)
    pos_c = (t % GRID_W).astype(F32)
    dd = d % 32
    fi = dd % quarter
    use_col = (d // 32) == 1
    first = dd < quarter
    ang = jnp.where(use_col[None, :], pos_c[:, None], pos_r[:, None]) * freqs[fi][None, :]
    cos = jnp.cos(ang)
    sin = jnp.where(first[None, :], -jnp.sin(ang), jnp.sin(ang))
    pad = t_total - n
    cos = jnp.concatenate([cos, jnp.ones((pad, LANES), F32)], axis=0)
    sin = jnp.concatenate([sin, jnp.zeros((pad, LANES), F32)], axis=0)
    return cos, sin


def _ada_kernel(c_ref, w_ref, b_ref, o_ref):
    cs = _silu(c_ref[...])
    o_ref[0] = jnp.dot(cs, w_ref[0], preferred_element_type=F32) + b_ref[0]


def _ada_call(c_rows, ada_w, ada_b):
    depth = ada_w.shape[0]
    rows = c_rows.shape[0]
    nb = 3
    return pl.pallas_call(
        _ada_kernel,
        grid=(depth, nb),
        in_specs=[
            pl.BlockSpec((rows, D_MODEL), lambda l, j: (0, 0)),
            pl.BlockSpec((1, D_MODEL, D_MODEL), lambda l, j: (l, 0, j)),
            pl.BlockSpec((1, 1, D_MODEL), lambda l, j: (l, 0, j)),
        ],
        out_specs=pl.BlockSpec((1, rows, D_MODEL), lambda l, j: (l, 0, j)),
        out_shape=jax.ShapeDtypeStruct((depth, rows, nb * D_MODEL), F32),
        compiler_params=_params(("parallel", "parallel")),
        name="ada_mod",
    )(c_rows, ada_w, ada_b.reshape(depth, 1, nb * D_MODEL))


def _head_mean_sq(t, mask_a):
    sq = t * t
    s_all = jnp.sum(sq, axis=-1, keepdims=True)
    s_a = jnp.sum(jnp.where(mask_a, sq, 0.0), axis=-1, keepdims=True)
    return jnp.where(mask_a, s_a, s_all - s_a) * (1.0 / HEAD_D)


def _norm_rope(t, w_lane, cos, sin, mask_a):
    tn = t * lax.rsqrt(_head_mean_sq(t, mask_a) + EPS) * w_lane
    return tn * cos + pltpu.roll(tn, LANES // 2, 1) * sin


def _inproj_kernel(x_ref, mod_ref, nw_ref, w_ref, cos_ref, sin_ref, qw_ref, kw_ref,
                   rq_ref, rk_ref, rv_ref, rg_ref, nq_ref, nk_ref, nv_ref, ng_ref,
                   gq_ref, gk_ref, gv_ref, gg_ref, mg_ref):
    x = x_ref[0]
    ms = jnp.mean(x * x, axis=-1, keepdims=True)
    y = x * lax.rsqrt(ms + EPS) * nw_ref[...]
    mod = mod_ref[0]
    shift = mod[:, :D_MODEL]
    scale = mod[:, D_MODEL:2 * D_MODEL]
    h = (y * (1.0 + scale) + shift).astype(BF16)

    def proj(off, width):
        return jnp.dot(h, w_ref[:, off:off + width], preferred_element_type=F32)

    qk_scale = HEAD_D ** -0.5
    rq_ref[0] = proj(O_RQ, RET_W).astype(BF16)
    rk_ref[0] = proj(O_RK, RET_W).astype(BF16)
    rv_ref[0] = proj(O_RV, RET_W).astype(BF16)
    rg_ref[0] = _silu(proj(O_RG, RET_W)).astype(BF16)
    nq_ref[0] = (proj(O_NQ, NA_W) * qk_scale).astype(BF16)
    nk_ref[0] = proj(O_NK, NA_W).astype(BF16)
    nv_ref[0] = proj(O_NV, NA_W).astype(BF16)
    ng_ref[0] = _silu(proj(O_NG, NA_W)).astype(BF16)

    lane = lax.broadcasted_iota(jnp.int32, (1, LANES), 1)
    mask_a = (lane % 64) < 32
    cos = cos_ref[...]
    sin = sin_ref[...]
    qw = qw_ref[...]
    for j in range(GQA_W // LANES):
        t = proj(O_GQ + j * LANES, LANES)
        gq_ref[0, :, j * LANES:(j + 1) * LANES] = (_norm_rope(t, qw, cos, sin, mask_a)
                                                   * (qk_scale * LOG2E)).astype(BF16)
    gk_ref[0] = _norm_rope(proj(O_GK, GQA_KV_W), kw_ref[...], cos, sin, mask_a).astype(BF16)
    gv_ref[0] = proj(O_GV, GQA_KV_W).astype(BF16)
    gg_ref[0] = _silu(proj(O_GG, GQA_W)).astype(BF16)
    for j in range(N_BRANCH):
        mg_ref[0, :, j * D_MODEL:(j + 1) * D_MODEL] = _sigmoid(proj(O_MG + j * D_MODEL, D_MODEL)).astype(BF16)


def _mod_index(n_lat_blocks, batch):
    return lambda b, i: (jnp.where(i < n_lat_blocks, b, batch), 0, 0)


def _inproj_call(xa, mod, norm_w, w_in, cos, sin, qw, kw, n_lat):
    b, t, _ = xa.shape
    tm = ROW_BLOCK
    widths = (RET_W, RET_W, RET_W, RET_W, NA_W, NA_W, NA_W, NA_W, GQA_W, GQA_KV_W, GQA_KV_W, GQA_W,
              N_BRANCH * D_MODEL)
    tok = lambda w: pl.BlockSpec((1, tm, w), lambda bi, i: (bi, i, 0))
    const2 = lambda shape: pl.BlockSpec(shape, lambda bi, i: (0, 0))
    return pl.pallas_call(
        _inproj_kernel,
        grid=(b, t // tm),
        in_specs=[
            tok(D_MODEL),
            pl.BlockSpec((1, 1, 3 * D_MODEL), _mod_index(n_lat // tm, b)),
            const2((1, D_MODEL)),
            pl.BlockSpec((D_MODEL, IN_COLS), lambda bi, i: (0, 0), pipeline_mode=pl.Buffered(1)),
            pl.BlockSpec((tm, LANES), lambda bi, i: (i, 0)),
            pl.BlockSpec((tm, LANES), lambda bi, i: (i, 0)),
            const2((1, LANES)),
            const2((1, LANES)),
        ],
        out_specs=[tok(w) for w in widths],
        out_shape=[jax.ShapeDtypeStruct((b, t, w), BF16) for w in widths],
        compiler_params=_params(("parallel", "parallel")),
        name="in_proj",
    )(xa, mod, norm_w, w_in, cos, sin, qw, kw)


def _ret_kernel(rld_ref, q_ref, k_ref, v_ref, o_ref, state, dec, xi, zeta, gch):
    d = pl.program_id(1)
    s = pl.program_id(2)
    c = RET_CHUNK
    ksc = RET_D ** -0.5

    @pl.when(s == 0)
    def _():
        state[...] = jnp.zeros_like(state)
        ii = lax.broadcasted_iota(jnp.int32, (c, c), 0)
        jj = lax.broadcasted_iota(jnp.int32, (c, c), 1)
        delta = jnp.where(d == 0, ii - jj, jj - ii).astype(F32)
        ip = jnp.where(d == 0, ii, c - 1 - ii).astype(F32)
        for h in range(RET_HEADS):
            lg = -jnp.exp(jnp.full((c, c), rld_ref[d, h], F32))
            dec[h] = jnp.where(delta >= 0, jnp.exp(jnp.maximum(delta, 0.0) * lg), 0.0) * ksc
            xi[h] = jnp.exp((ip + 1.0) * lg)
            zeta[h] = jnp.exp((c - 1.0 - ip) * lg) * ksc
            gch[h] = jnp.exp(float(c) * lg)

    for h in range(RET_HEADS):
        sl = slice(h * RET_D, (h + 1) * RET_D)
        q = q_ref[0, :, sl]
        k = k_ref[0, :, sl]
        v = v_ref[0, :, sl]
        scores = lax.dot_general(q, k, (((1,), (1,)), ((), ())), preferred_element_type=F32)
        intra = jnp.dot((scores * dec[h]).astype(BF16), v, preferred_element_type=F32)
        st = state[h]
        inter = jnp.dot(q, st.astype(BF16), preferred_element_type=F32) * xi[h]
        o_ref[0, 0, :, sl] = intra + inter
        vz = (v.astype(F32) * zeta[h]).astype(BF16)
        upd = lax.dot_general(k, vz, (((0,), (0,)), ((), ())), preferred_element_type=F32)
        state[h] = gch[h] * st + upd


def _ret_call(rld, rq, rk, rv, n_lat):
    b, t, _ = rq.shape
    c = RET_CHUNK
    nc = t // c
    ncn = n_lat // c

    def pos(d, s):
        return jnp.where(d == 0, (s + ncn) % nc, nc - 1 - s)

    tok = pl.BlockSpec((1, c, RET_W), lambda bi, d, s: (bi, pos(d, s), 0))
    sq = pltpu.VMEM((RET_HEADS, c, c), F32)
    return pl.pallas_call(
        _ret_kernel,
        grid=(b, 2, nc),
        in_specs=[pl.BlockSpec(memory_space=pltpu.SMEM), tok, tok, tok],
        out_specs=pl.BlockSpec((1, 1, c, RET_W), lambda bi, d, s: (d, bi, pos(d, s), 0)),
        out_shape=jax.ShapeDtypeStruct((2, b, t, RET_W), F32),
        scratch_shapes=[sq, sq, sq, sq, sq],
        compiler_params=_params(("parallel", "arbitrary", "arbitrary")),
        name="retention",
    )(rld, rq, rk, rv)


def _pair_split(q, mask_a):
    zero = jnp.zeros_like(q)
    return jnp.concatenate([jnp.where(mask_a, q, zero), jnp.where(mask_a, zero, q)], axis=0)


def _dot_t(a, b):
    return lax.dot_general(a, b, (((1,), (1,)), ((), ())), preferred_element_type=F32)


def _na_kernel(q_ref, k0_ref, k1_ref, k2_ref, v0_ref, v1_ref, v2_ref, kc_ref, vc_ref, bias_ref, g_ref, o_ref):
    m = q_ref.shape[1]
    lane = lax.broadcasted_iota(jnp.int32, (1, LANES), 1)
    lo = lane < 64
    q2 = _pair_split(q_ref[0], lo)
    bias = jnp.concatenate([bias_ref[0, 0, 0], bias_ref[0, 0, 1]], axis=0)
    u = ROW_BLOCK
    s_loc = [_dot_t(q2, kr[0]) + bias[:, i * u:(i + 1) * u] for i, kr in enumerate((k0_ref, k1_ref, k2_ref))]
    s_ctx = _dot_t(q2, kc_ref[0])
    mx = jnp.max(s_ctx, axis=-1, keepdims=True)
    for sl in s_loc:
        mx = jnp.maximum(mx, jnp.max(sl, axis=-1, keepdims=True))
    p_ctx = jnp.exp(s_ctx - mx)
    den = jnp.sum(p_ctx, axis=-1, keepdims=True)
    acc = jnp.dot(p_ctx.astype(BF16), vc_ref[0], preferred_element_type=F32)
    for sl, vr in zip(s_loc, (v0_ref, v1_ref, v2_ref)):
        p = jnp.exp(sl - mx)
        den = den + jnp.sum(p, axis=-1, keepdims=True)
        acc = acc + jnp.dot(p.astype(BF16), vr[0], preferred_element_type=F32)
    o2 = acc / den
    o = jnp.where(lo, o2[:m], o2[m:])
    o_ref[0] = (o * g_ref[0].astype(F32)).astype(BF16)


def _na_bias_tables(rpb, rows):
    ur = NA_UNIT_ROWS
    g_units = rows // ur
    n_dr = 2 * NA_WIN_ROWS - 1
    n_dc = 2 * NA_WIN_COLS - 1
    qc = np.arange(GRID_W)
    cstart = np.clip(qc - NA_WIN_COLS // 2, 0, GRID_W - NA_WIN_COLS)
    col_ok = (qc[None, :] >= cstart[:, None]) & (qc[None, :] < cstart[:, None] + NA_WIN_COLS)
    dc = qc[None, :] - qc[:, None] + NA_WIN_COLS - 1
    hot_c = ((dc[None] == np.arange(n_dc)[:, None, None]) & col_ok[None]).astype(np.float32)
    rho = np.arange(ur)
    kappa = np.arange(NA_WIN_UNITS * ur)
    hot_r = np.zeros((4, ur, NA_WIN_UNITS * ur, n_dr), np.float32)
    for case, g in enumerate((0, 1, g_units - 1)):
        r = g * ur + rho
        kr = int(np.clip(g - 1, 0, g_units - NA_WIN_UNITS)) * ur + kappa
        rs = np.clip(r - NA_WIN_ROWS // 2, 0, rows - NA_WIN_ROWS)
        row_ok = (kr[None, :] >= rs[:, None]) & (kr[None, :] < rs[:, None] + NA_WIN_ROWS)
        dr = kr[None, :] - r[:, None] + NA_WIN_ROWS - 1
        hot_r[case] = (dr[..., None] == np.arange(n_dr)) & row_ok[..., None]
    ok = (hot_r.sum(-1)[:, :, None, :, None] * col_ok[None, None, :, None, :]) > 0
    hi = lax.Precision.HIGHEST
    toep = jnp.einsum('lhrd,dqj->lhrqj', rpb.astype(F32), jnp.asarray(hot_c), precision=hi)
    big = jnp.einsum('cpkr,lhrqj->lchpqkj', jnp.asarray(hot_r), toep, precision=hi)
    big = jnp.where(jnp.asarray(ok)[None, :, None], big, MASKED)
    depth = rpb.shape[0]
    return big.reshape(depth, 4, NA_HEADS // 2, 2, ROW_BLOCK, NA_WIN_UNITS * ROW_BLOCK)


def _na_call(nq, nk, nv, ng, bias, n_lat, n_ctx):
    b, t, _ = nq.shape
    u = ROW_BLOCK
    g_units = n_lat // u
    n_units = t // u
    pairs = NA_W // LANES

    def win(i):
        return lambda bi, j, g: (bi, jnp.clip(g - 1, 0, g_units - NA_WIN_UNITS) + i, j)

    def case(bi, j, g):
        cs = jnp.where(g == 0, 0, jnp.where(g < g_units - 1, 1, jnp.where(g == g_units - 1, 2, 3)))
        return (cs, j, 0, 0, 0)

    tok = pl.BlockSpec((1, u, LANES), lambda bi, j, g: (bi, g, j))
    ctx = pl.BlockSpec((1, n_ctx, LANES), lambda bi, j, g: (bi, n_lat // n_ctx, j))
    kv = [pl.BlockSpec((1, u, LANES), win(i)) for i in range(NA_WIN_UNITS)]
    return pl.pallas_call(
        _na_kernel,
        grid=(b, pairs, n_units),
        in_specs=[tok] + kv + kv + [ctx, ctx,
                                    pl.BlockSpec((1, 1, 2, u, NA_WIN_UNITS * u), case), tok],
        out_specs=tok,
        out_shape=jax.ShapeDtypeStruct((b, t, NA_W), BF16),
        compiler_params=_params(("parallel", "parallel", "arbitrary")),
        name="nbr_attn",
    )(nq, nk, nk, nk, nv, nv, nv, nk, nv, bias, ng)


def _gqa_kernel(q_ref, k_ref, v_ref, g_ref, o_ref, q2_sc, m_sc, l_sc, acc_sc, *, n_lat, tk):
    i = pl.program_id(1)
    tq = q_ref.shape[1]
    n_ctx = k_ref.shape[1] - n_lat
    pairs = GQA_W // LANES
    lane = lax.broadcasted_iota(jnp.int32, (1, LANES), 1)
    mask_a = (lane % 64) < 32
    lo = lane < 64
    for j in range(pairs):
        q2_sc[j] = _pair_split(q_ref[0, :, j * LANES:(j + 1) * LANES], mask_a)
    m_sc[...] = jnp.full_like(m_sc, MASKED)
    l_sc[...] = jnp.zeros_like(l_sc)
    acc_sc[...] = jnp.zeros_like(acc_sc)

    def step(k, v):
        width = k.shape[0]
        for j in range(pairs):
            s = _dot_t(q2_sc[j], k)
            m_old = m_sc[j]
            m_new = jnp.maximum(m_old, jnp.max(s, axis=-1, keepdims=True))
            alpha = jnp.exp2(m_old - m_new)
            p = jnp.exp2(s - pltpu.repeat(m_new, width // LANES, axis=1))
            l_sc[j] = alpha * l_sc[j] + jnp.sum(p, axis=-1, keepdims=True)
            acc_sc[j] = alpha * acc_sc[j] + jnp.dot(p.astype(BF16), v, preferred_element_type=F32)
            m_sc[j] = m_new

    @pl.when(i * tq < n_lat)
    def _():
        def body(c, carry):
            off = pl.multiple_of(c * tk, tk)
            step(k_ref[0, pl.ds(off, tk), :], v_ref[0, pl.ds(off, tk), :])
            return carry
        lax.fori_loop(0, n_lat // tk, body, 0)

    step(k_ref[0, n_lat:n_lat + n_ctx, :], v_ref[0, n_lat:n_lat + n_ctx, :])
    for j in range(pairs):
        sl = slice(j * LANES, (j + 1) * LANES)
        o2 = acc_sc[j] / l_sc[j]
        o = jnp.where(lo, o2[:tq], o2[tq:])
        o_ref[0, :, sl] = (o * g_ref[0, :, sl].astype(F32)).astype(BF16)


def _gqa_call(gq, gk, gv, gg, n_lat):
    b, t, _ = gq.shape
    tq = ROW_BLOCK
    tk = GQA_KEY_CHUNK
    pairs = GQA_W // LANES
    tok = pl.BlockSpec((1, tq, GQA_W), lambda bi, i: (bi, i, 0))
    kv = pl.BlockSpec((1, t, GQA_KV_W), lambda bi, i: (bi, 0, 0))
    return pl.pallas_call(
        functools.partial(_gqa_kernel, n_lat=n_lat, tk=tk),
        grid=(b, t // tq),
        in_specs=[tok, kv, kv, tok],
        out_specs=tok,
        out_shape=jax.ShapeDtypeStruct((b, t, GQA_W), BF16),
        scratch_shapes=[pltpu.VMEM((pairs, 2 * tq, LANES), BF16)] + [pltpu.VMEM((pairs, 2 * tq, LANES), F32)] * 3,
        compiler_params=_params(("parallel", "arbitrary")),
        name="gqa_attn",
    )(gq, gk, gv, gg)


def _merge_kernel(x_ref, mod_ref, of_ref, ob_ref, rg_ref, gnw_ref, na_ref, gqa_ref, mg_ref,
                  wr_ref, wn_ref, wg_ref, wo_ref, fw_ref, o_ref, *, final):
    o = of_ref[0, 0] + ob_ref[0, 0]
    parts = []
    for h in range(RET_HEADS):
        oh = o[:, h * RET_D:(h + 1) * RET_D]
        mu = jnp.mean(oh, axis=-1, keepdims=True)
        dev = oh - mu
        var = jnp.mean(dev * dev, axis=-1, keepdims=True)
        parts.append(dev * lax.rsqrt(var + EPS))
    ret_x = (jnp.concatenate(parts, axis=-1) * gnw_ref[...] * rg_ref[0].astype(F32)).astype(BF16)
    m = mg_ref[0, :, :D_MODEL].astype(F32) * jnp.dot(ret_x, wr_ref[...], preferred_element_type=F32)
    m = m + mg_ref[0, :, D_MODEL:2 * D_MODEL].astype(F32) * jnp.dot(na_ref[0], wn_ref[...],
                                                                    preferred_element_type=F32)
    m = m + mg_ref[0, :, 2 * D_MODEL:].astype(F32) * jnp.dot(gqa_ref[0], wg_ref[...],
                                                             preferred_element_type=F32)
    out = jnp.dot(m.astype(BF16), wo_ref[...], preferred_element_type=F32)
    gate = mod_ref[0][:, 2 * D_MODEL:]
    xn = x_ref[0] + gate * out
    if final:
        ms = jnp.mean(xn * xn, axis=-1, keepdims=True)
        xn = xn * lax.rsqrt(ms + EPS) * fw_ref[...]
    o_ref[0] = xn


def _merge_call(xa, mod, ret_o, rg, gnw, na_o, gqa_o, mg, wr, wn, wg, wo, fw, n_lat, final):
    b, t, _ = xa.shape
    tm = ROW_BLOCK
    t_out = n_lat if final else t
    tok = lambda w: pl.BlockSpec((1, tm, w), lambda bi, i: (bi, i, 0))
    const2 = lambda shape: pl.BlockSpec(shape, lambda bi, i: (0, 0))
    return pl.pallas_call(
        functools.partial(_merge_kernel, final=final),
        grid=(b, t_out // tm),
        in_specs=[
            tok(D_MODEL),
            pl.BlockSpec((1, 1, 3 * D_MODEL), _mod_index(n_lat // tm, b)),
            pl.BlockSpec((1, 1, tm, RET_W), lambda bi, i: (0, bi, i, 0)),
            pl.BlockSpec((1, 1, tm, RET_W), lambda bi, i: (1, bi, i, 0)),
            tok(RET_W),
            const2((1, RET_W)),
            tok(NA_W),
            tok(GQA_W),
            tok(N_BRANCH * D_MODEL),
            const2((RET_W, D_MODEL)),
            const2((NA_W, D_MODEL)),
            const2((GQA_W, D_MODEL)),
            const2((D_MODEL, D_MODEL)),
            const2((1, D_MODEL)),
        ],
        out_specs=tok(D_MODEL),
        out_shape=jax.ShapeDtypeStruct((b, t_out, D_MODEL), F32),
        compiler_params=_params(("parallel", "parallel")),
        name="merge",
    )(xa, mod, ret_o, ret_o, rg, gnw, na_o, gqa_o, mg, wr, wn, wg, wo, fw)


def _forward(x, c, ctx, c_ctx, ada_w, ada_b, norm_w, w_in, ret_log_decay, ret_gn_w, na_rpb,
             q_norm_w, k_norm_w, w_ret_o, w_na_o, w_gqa_o, w_out, final_norm_w):
    b, n, _ = x.shape
    n_ctx = ctx.shape[1]
    depth = ada_w.shape[0]
    t = n + n_ctx
    rows = n // GRID_W
    assert n % ROW_BLOCK == 0 and n_ctx == ROW_BLOCK and n % n_ctx == 0 and n % GQA_KEY_CHUNK == 0
    assert rows >= NA_WIN_UNITS * NA_UNIT_ROWS and rows % NA_UNIT_ROWS == 0

    w_in_p = _permute_w_in(w_in).astype(BF16)
    qw = _k_layout(jnp.concatenate([q_norm_w, q_norm_w], axis=-1)).reshape(depth, 1, LANES)
    kw = _k_layout(jnp.concatenate([k_norm_w, k_norm_w], axis=-1)).reshape(depth, 1, LANES)
    w_ret_p = w_ret_o.astype(BF16)
    w_na_p = w_na_o.astype(BF16)
    w_gqa_p = _o_layout(w_gqa_o, axis=-2).astype(BF16)
    w_out_p = w_out.astype(BF16)
    na_bias = _na_bias_tables(na_rpb, rows)

    cos, sin = _rope_tables(n, t)
    mod_rows = 8 * ((b + 1 + 7) // 8)
    c_rows = jnp.concatenate([c, c_ctx[None, :], jnp.zeros((mod_rows - b - 1, D_MODEL), F32)], axis=0)
    mod_all = _ada_call(c_rows, ada_w, ada_b)

    xa = jnp.concatenate([x, ctx], axis=1)
    for layer in range(depth):
        final = layer == depth - 1
        mod = mod_all[layer].reshape(mod_rows, 1, 3 * D_MODEL)
        (rq, rk, rv, rg, nq, nk, nv, ng, gq, gk, gv, gg, mg) = _inproj_call(
            xa, mod, norm_w[layer].reshape(1, D_MODEL), w_in_p[layer], cos, sin, qw[layer], kw[layer], n)
        ret_o = _ret_call(ret_log_decay[layer], rq, rk, rv, n)
        na_o = _na_call(nq, nk, nv, ng, na_bias[layer], n, n_ctx)
        gqa_o = _gqa_call(gq, gk, gv, gg, n)
        xa = _merge_call(xa, mod, ret_o, rg, ret_gn_w[layer].reshape(1, RET_W), na_o, gqa_o, mg,
                         w_ret_p[layer], w_na_p[layer], w_gqa_p[layer], w_out_p[layer],
                         final_norm_w.reshape(1, D_MODEL), n, final)
    return xa


def kernel(x, c, ctx, c_ctx, ada_w, ada_b, norm_w, w_in, ret_log_decay, ret_gn_w, na_rpb, q_norm_w, k_norm_w,
           w_ret_o, w_na_o, w_gqa_o, w_out, final_norm_w):
    return _forward(x, c, ctx, c_ctx, ada_w, ada_b, norm_w, w_in, ret_log_decay, ret_gn_w, na_rpb,
                    q_norm_w, k_norm_w, w_ret_o, w_na_o, w_gqa_o, w_out, final_norm_w)
```

```python
import functools

import numpy as np
import jax
import jax.numpy as jnp
from jax import lax
from jax.experimental import pallas as pl
from jax.experimental.pallas import tpu as pltpu

F32 = jnp.float32
BF16 = jnp.bfloat16

D_MODEL = 1024
GRID_W = 64
EPS = 1e-6
ROPE_THETA = 10000.0
RET_HEADS = 4
RET_D = 128
RET_W = RET_HEADS * RET_D
RET_CHUNK = 128
NA_HEADS = 8
HEAD_D = 64
NA_W = NA_HEADS * HEAD_D
NA_WIN_ROWS = 8
NA_WIN_COLS = 16
GQA_HEADS = 8
GQA_KV_HEADS = 2
GQA_W = GQA_HEADS * HEAD_D
GQA_KV_W = GQA_KV_HEADS * HEAD_D
N_BRANCH = 3
LANES = 128
ROW_BLOCK = 256
NA_UNIT_ROWS = ROW_BLOCK // GRID_W
NA_WIN_UNITS = 3
GQA_KEY_CHUNK = 1024
LOG2E = 1.4426950408889634
GQA_SHIFT_LIMIT = 40.0
MASKED = -1e30
VMEM_LIMIT = 56 * 1024 * 1024

_SPLITS = (RET_W, RET_W, RET_W, RET_W, NA_W, NA_W, NA_W, NA_W, GQA_W, GQA_KV_W, GQA_KV_W, GQA_W,
           N_BRANCH * D_MODEL)
_OFFS = np.concatenate([[0], np.cumsum(_SPLITS)])
IN_COLS = int(_OFFS[-1])
(O_RQ, O_RK, O_RV, O_RG, O_NQ, O_NK, O_NV, O_NG, O_GQ, O_GK, O_GV, O_GG, O_MG) = (int(o) for o in _OFFS[:-1])


def _params(sem, vmem=VMEM_LIMIT):
    return pltpu.CompilerParams(dimension_semantics=sem, vmem_limit_bytes=vmem)


def _silu(x):
    return x / (1.0 + jnp.exp(-x))


def _sigmoid(x):
    return 1.0 / (1.0 + jnp.exp(-x))


def _lane_dims():
    l = np.arange(LANES)
    half = l // 64
    slot = (l % 64) // 32
    r = l % 32
    d = np.where(r < 16, r, 32 + r - 16) + 16 * half
    return slot, d


def _transpose_cols(w, shape, order, axis=-1):
    axis = axis % w.ndim
    k = len(shape)
    full = w.shape[:axis] + tuple(shape) + w.shape[axis + 1:]
    perm = tuple(range(axis)) + tuple(axis + o for o in order) + tuple(range(axis + k, len(full)))
    return jnp.transpose(w.reshape(full), perm).reshape(w.shape)


def _q_layout(w):
    return _transpose_cols(w, (2, GQA_HEADS // 2, 2, 2, 16), (1, 3, 0, 2, 4))


def _k_layout(w):
    return _transpose_cols(w, (2, 2, 2, 16), (2, 0, 1, 3))


def _o_layout(w, axis=-1):
    return _transpose_cols(w, (2, GQA_HEADS // 2, HEAD_D), (1, 0, 2), axis=axis)


def _permute_w_in(w_in):
    parts = [w_in[..., :O_GQ], _q_layout(w_in[..., O_GQ:O_GK]), _k_layout(w_in[..., O_GK:O_GV]),
             w_in[..., O_GV:O_GG], _o_layout(w_in[..., O_GG:O_MG]), w_in[..., O_MG:]]
    return jnp.concatenate(parts, axis=-1)


def _rope_tables(n, t_total):
    _, d = _lane_dims()
    quarter = HEAD_D // 4
    freqs = ROPE_THETA ** (-jnp.arange(quarter, dtype=F32) / quarter)
    t = jnp.arange(n)
    pos_r = (t // GRID_W).astype(F32)
    pos_c = (t % GRID_W).astype(F32)
    dd = d % 32
    fi = dd % quarter
    use_col = (d // 32) == 1
    first = dd < quarter
    ang = jnp.where(use_col[None, :], pos_c[:, None], pos_r[:, None]) * freqs[fi][None, :]
    cos = jnp.cos(ang)
    sin = jnp.where(first[None, :], -jnp.sin(ang), jnp.sin(ang))
    pad = t_total - n
    cos = jnp.concatenate([cos, jnp.ones((pad, LANES), F32)], axis=0)
    sin = jnp.concatenate([sin, jnp.zeros((pad, LANES), F32)], axis=0)
    return cos, sin


def _ada_kernel(c_ref, w_ref, b_ref, o_ref):
    cs = _silu(c_ref[...])
    o_ref[0] = jnp.dot(cs, w_ref[0], preferred_element_type=F32) + b_ref[0]


def _ada_call(c_rows, ada_w, ada_b):
    depth = ada_w.shape[0]
    rows = c_rows.shape[0]
    nb = 3
    return pl.pallas_call(
        _ada_kernel,
        grid=(depth, nb),
        in_specs=[
            pl.BlockSpec((rows, D_MODEL), lambda l, j: (0, 0)),
            pl.BlockSpec((1, D_MODEL, D_MODEL), lambda l, j: (l, 0, j)),
            pl.BlockSpec((1, 1, D_MODEL), lambda l, j: (l, 0, j)),
        ],
        out_specs=pl.BlockSpec((1, rows, D_MODEL), lambda l, j: (l, 0, j)),
        out_shape=jax.ShapeDtypeStruct((depth, rows, nb * D_MODEL), F32),
        compiler_params=_params(("parallel", "parallel")),
        name="ada_mod",
    )(c_rows, ada_w, ada_b.reshape(depth, 1, nb * D_MODEL))


def _head_mean_sq(t, mask_a):
    sq = t * t
    s_all = jnp.sum(sq, axis=-1, keepdims=True)
    s_a = jnp.sum(jnp.where(mask_a, sq, 0.0), axis=-1, keepdims=True)
    return jnp.where(mask_a, s_a, s_all - s_a) * (1.0 / HEAD_D)


def _norm_rope(t, w_lane, cos, sin, mask_a):
    tn = t * lax.rsqrt(_head_mean_sq(t, mask_a) + EPS) * w_lane
    return tn * cos + pltpu.roll(tn, LANES // 2, 1) * sin


def _inproj_kernel(x_ref, mod_ref, nw_ref, w_ref, cos_ref, sin_ref, qw_ref, kw_ref,
                   rq_ref, rk_ref, rv_ref, rg_ref, nq_ref, nk_ref, nv_ref, ng_ref,
                   gq_ref, gk_ref, gv_ref, gg_ref, mg_ref):
    x = x_ref[0]
    ms = jnp.mean(x * x, axis=-1, keepdims=True)
    y = x * lax.rsqrt(ms + EPS) * nw_ref[...]
    mod = mod_ref[0]
    shift = mod[:, :D_MODEL]
    scale = mod[:, D_MODEL:2 * D_MODEL]
    h = (y * (1.0 + scale) + shift).astype(BF16)

    def proj(off, width):
        return jnp.dot(h, w_ref[:, off:off + width], preferred_element_type=F32)

    qk_scale = HEAD_D ** -0.5
    rq_ref[0] = proj(O_RQ, RET_W).astype(BF16)
    rk_ref[0] = proj(O_RK, RET_W).astype(BF16)
    rv_ref[0] = proj(O_RV, RET_W).astype(BF16)
    rg_ref[0] = _silu(proj(O_RG, RET_W)).astype(BF16)
    nq_ref[0] = (proj(O_NQ, NA_W) * (qk_scale * LOG2E)).astype(BF16)
    nk_ref[0] = proj(O_NK, NA_W).astype(BF16)
    nv_ref[0] = proj(O_NV, NA_W).astype(BF16)
    ng_ref[0] = _silu(proj(O_NG, NA_W)).astype(BF16)

    lane = lax.broadcasted_iota(jnp.int32, (1, LANES), 1)
    mask_a = (lane % 64) < 32
    cos = cos_ref[...]
    sin = sin_ref[...]
    qw = qw_ref[...]
    for j in range(GQA_W // LANES):
        t = proj(O_GQ + j * LANES, LANES)
        gq_ref[0, :, j * LANES:(j + 1) * LANES] = (_norm_rope(t, qw, cos, sin, mask_a)
                                                   * (qk_scale * LOG2E)).astype(BF16)
    gk_ref[0] = _norm_rope(proj(O_GK, GQA_KV_W), kw_ref[...], cos, sin, mask_a).astype(BF16)
    gv_ref[0] = proj(O_GV, GQA_KV_W).astype(BF16)
    gg_ref[0] = _silu(proj(O_GG, GQA_W)).astype(BF16)
    for j in range(N_BRANCH):
        mg_ref[0, :, j * D_MODEL:(j + 1) * D_MODEL] = _sigmoid(proj(O_MG + j * D_MODEL, D_MODEL)).astype(BF16)


def _mod_index(n_lat_blocks, batch):
    return lambda b, i: (jnp.where(i < n_lat_blocks, b, batch), 0, 0)


def _inproj_call(xa, mod, norm_w, w_in, cos, sin, qw, kw, n_lat):
    b, t, _ = xa.shape
    tm = ROW_BLOCK
    widths = (RET_W, RET_W, RET_W, RET_W, NA_W, NA_W, NA_W, NA_W, GQA_W, GQA_KV_W, GQA_KV_W, GQA_W,
              N_BRANCH * D_MODEL)
    tok = lambda w: pl.BlockSpec((1, tm, w), lambda bi, i: (bi, i, 0))
    const2 = lambda shape: pl.BlockSpec(shape, lambda bi, i: (0, 0))
    return pl.pallas_call(
        _inproj_kernel,
        grid=(b, t // tm),
        in_specs=[
            tok(D_MODEL),
            pl.BlockSpec((1, 1, 3 * D_MODEL), _mod_index(n_lat // tm, b)),
            const2((1, D_MODEL)),
            pl.BlockSpec((D_MODEL, IN_COLS), lambda bi, i: (0, 0), pipeline_mode=pl.Buffered(1)),
            pl.BlockSpec((tm, LANES), lambda bi, i: (i, 0)),
            pl.BlockSpec((tm, LANES), lambda bi, i: (i, 0)),
            const2((1, LANES)),
            const2((1, LANES)),
        ],
        out_specs=[tok(w) for w in widths],
        out_shape=[jax.ShapeDtypeStruct((b, t, w), BF16) for w in widths],
        compiler_params=_params(("parallel", "parallel")),
        name="in_proj",
    )(xa, mod, norm_w, w_in, cos, sin, qw, kw)


def _ret_kernel(rld_ref, qf_ref, kf_ref, vf_ref, qb_ref, kb_ref, vb_ref, of_ref, ob_ref,
                state, dec, xi, zeta, gch):
    s = pl.program_id(1)
    c = RET_CHUNK
    ksc = RET_D ** -0.5

    @pl.when(s == 0)
    def _():
        state[...] = jnp.zeros_like(state)
        ii = lax.broadcasted_iota(jnp.int32, (c, c), 0)
        jj = lax.broadcasted_iota(jnp.int32, (c, c), 1)
        for d in range(2):
            delta = ((ii - jj) if d == 0 else (jj - ii)).astype(F32)
            ip = (ii if d == 0 else c - 1 - ii).astype(F32)
            for h in range(RET_HEADS):
                lg = -jnp.exp(jnp.full((c, c), rld_ref[d, h], F32))
                dec[d, h] = jnp.where(delta >= 0, jnp.exp(jnp.maximum(delta, 0.0) * lg), 0.0) * ksc
                xi[d, h] = jnp.exp((ip + 1.0) * lg)
                zeta[d, h] = jnp.exp((c - 1.0 - ip) * lg) * ksc
                gch[d, h] = jnp.exp(float(c) * lg)

    for d, (q_ref, k_ref, v_ref, o_ref) in enumerate(((qf_ref, kf_ref, vf_ref, of_ref),
                                                      (qb_ref, kb_ref, vb_ref, ob_ref))):
        for h in range(RET_HEADS):
            sl = slice(h * RET_D, (h + 1) * RET_D)
            q = q_ref[0, :, sl]
            k = k_ref[0, :, sl]
            v = v_ref[0, :, sl]
            scores = lax.dot_general(q, k, (((1,), (1,)), ((), ())), preferred_element_type=F32)
            intra = jnp.dot((scores * dec[d, h]).astype(BF16), v, preferred_element_type=F32)
            st = state[d, h]
            inter = jnp.dot(q, st.astype(BF16), preferred_element_type=F32) * xi[d, h]
            o_ref[0, :, sl] = intra + inter
            vz = (v.astype(F32) * zeta[d, h]).astype(BF16)
            upd = lax.dot_general(k, vz, (((0,), (0,)), ((), ())), preferred_element_type=F32)
            state[d, h] = gch[d, h] * st + upd


def _ret_call(rld, rq, rk, rv, n_lat):
    b, t, _ = rq.shape
    c = RET_CHUNK
    nc = t // c
    ncn = n_lat // c
    fwd = pl.BlockSpec((1, c, RET_W), lambda bi, s: (bi, (s + ncn) % nc, 0))
    bwd = pl.BlockSpec((1, c, RET_W), lambda bi, s: (bi, nc - 1 - s, 0))
    sq = pltpu.VMEM((2, RET_HEADS, c, c), F32)
    out = jax.ShapeDtypeStruct((b, t, RET_W), F32)
    return pl.pallas_call(
        _ret_kernel,
        grid=(b, nc),
        in_specs=[pl.BlockSpec(memory_space=pltpu.SMEM), fwd, fwd, fwd, bwd, bwd, bwd],
        out_specs=[fwd, bwd],
        out_shape=[out, out],
        scratch_shapes=[sq, sq, sq, sq, sq],
        compiler_params=_params(("parallel", "arbitrary")),
        name="retention",
    )(rld, rq, rk, rv, rq, rk, rv)


def _pair_split(q, mask_a):
    zero = jnp.zeros_like(q)
    return jnp.concatenate([jnp.where(mask_a, q, zero), jnp.where(mask_a, zero, q)], axis=0)


def _dot_t(a, b):
    return lax.dot_general(a, b, (((1,), (1,)), ((), ())), preferred_element_type=F32)


def _na_kernel(q_ref, k0_ref, k1_ref, k2_ref, v0_ref, v1_ref, v2_ref, kc_ref, vc_ref, bias_ref, g_ref, o_ref):
    m = q_ref.shape[1]
    u = ROW_BLOCK
    n_ctx = kc_ref.shape[1]
    lane = lax.broadcasted_iota(jnp.int32, (1, LANES), 1)
    lo = lane < 64
    for j in range(NA_W // LANES):
        sl = slice(j * LANES, (j + 1) * LANES)
        q2 = _pair_split(q_ref[0, :, sl], lo)
        bias = jnp.concatenate([bias_ref[0, j, 0], bias_ref[0, j, 1]], axis=0)
        s = jnp.concatenate(
            [_dot_t(q2, kc_ref[0, :, sl])]
            + [_dot_t(q2, kr[0, :, sl]) + bias[:, i * u:(i + 1) * u] for i, kr in enumerate((k0_ref, k1_ref, k2_ref))],
            axis=1)
        p = jnp.exp2(s - jnp.max(s, axis=-1, keepdims=True))
        den = jnp.sum(p, axis=-1, keepdims=True)
        pb = p.astype(BF16)
        acc = jnp.dot(pb[:, :n_ctx], vc_ref[0, :, sl], preferred_element_type=F32)
        for i, vr in enumerate((v0_ref, v1_ref, v2_ref)):
            acc = acc + jnp.dot(pb[:, n_ctx + i * u:n_ctx + (i + 1) * u], vr[0, :, sl],
                                preferred_element_type=F32)
        o2 = acc / den
        o = jnp.where(lo, o2[:m], o2[m:])
        o_ref[0, :, sl] = (o * g_ref[0, :, sl].astype(F32)).astype(BF16)


def _na_bias_tables(rpb, rows):
    ur = NA_UNIT_ROWS
    g_units = rows // ur
    n_dr = 2 * NA_WIN_ROWS - 1
    n_dc = 2 * NA_WIN_COLS - 1
    qc = np.arange(GRID_W)
    cstart = np.clip(qc - NA_WIN_COLS // 2, 0, GRID_W - NA_WIN_COLS)
    col_ok = (qc[None, :] >= cstart[:, None]) & (qc[None, :] < cstart[:, None] + NA_WIN_COLS)
    dc = qc[None, :] - qc[:, None] + NA_WIN_COLS - 1
    hot_c = ((dc[None] == np.arange(n_dc)[:, None, None]) & col_ok[None]).astype(np.float32)
    rho = np.arange(ur)
    kappa = np.arange(NA_WIN_UNITS * ur)
    hot_r = np.zeros((4, ur, NA_WIN_UNITS * ur, n_dr), np.float32)
    for case, g in enumerate((0, 1, g_units - 1)):
        r = g * ur + rho
        kr = int(np.clip(g - 1, 0, g_units - NA_WIN_UNITS)) * ur + kappa
        rs = np.clip(r - NA_WIN_ROWS // 2, 0, rows - NA_WIN_ROWS)
        row_ok = (kr[None, :] >= rs[:, None]) & (kr[None, :] < rs[:, None] + NA_WIN_ROWS)
        dr = kr[None, :] - r[:, None] + NA_WIN_ROWS - 1
        hot_r[case] = (dr[..., None] == np.arange(n_dr)) & row_ok[..., None]
    ok = (hot_r.sum(-1)[:, :, None, :, None] * col_ok[None, None, :, None, :]) > 0
    hi = lax.Precision.HIGHEST
    toep = jnp.einsum('lhrd,dqj->lhrqj', rpb.astype(F32) * LOG2E, jnp.asarray(hot_c), precision=hi)
    big = jnp.einsum('cpkr,lhrqj->lchpqkj', jnp.asarray(hot_r), toep, precision=hi)
    big = jnp.where(jnp.asarray(ok)[None, :, None], big, MASKED)
    depth = rpb.shape[0]
    return big.reshape(depth, 4, NA_HEADS // 2, 2, ROW_BLOCK, NA_WIN_UNITS * ROW_BLOCK)


def _na_call(nq, nk, nv, ng, bias, n_lat, n_ctx):
    b, t, _ = nq.shape
    u = ROW_BLOCK
    g_units = n_lat // u
    n_units = t // u
    pairs = NA_W // LANES

    def win(i):
        return lambda bi, g: (bi, jnp.clip(g - 1, 0, g_units - NA_WIN_UNITS) + i, 0)

    def case(bi, g):
        cs = jnp.where(g == 0, 0, jnp.where(g < g_units - 1, 1, jnp.where(g == g_units - 1, 2, 3)))
        return (cs, 0, 0, 0, 0)

    tok = pl.BlockSpec((1, u, NA_W), lambda bi, g: (bi, g, 0))
    ctx = pl.BlockSpec((1, n_ctx, NA_W), lambda bi, g: (bi, n_lat // n_ctx, 0))
    kv = [pl.BlockSpec((1, u, NA_W), win(i)) for i in range(NA_WIN_UNITS)]
    return pl.pallas_call(
        _na_kernel,
        grid=(b, n_units),
        in_specs=[tok] + kv + kv + [ctx, ctx,
                                    pl.BlockSpec((1, pairs, 2, u, NA_WIN_UNITS * u), case), tok],
        out_specs=tok,
        out_shape=jax.ShapeDtypeStruct((b, t, NA_W), BF16),
        compiler_params=_params(("parallel", "arbitrary")),
        name="nbr_attn",
    )(nq, nk, nk, nk, nv, nv, nv, nk, nv, bias, ng)


def _gqa_kernel(bound_ref, q_ref, k_ref, v_ref, g_ref, o_ref, q2_sc, m_sc, l_sc, acc_sc, *, n_lat, tk):
    i = pl.program_id(1)
    tq = q_ref.shape[1]
    pairs = GQA_W // LANES
    lane = lax.broadcasted_iota(jnp.int32, (1, LANES), 1)
    mask_a = (lane % 64) < 32
    lo = lane < 64
    for j in range(pairs):
        q2_sc[j] = _pair_split(q_ref[0, :, j * LANES:(j + 1) * LANES], mask_a)
    t_all = k_ref.shape[1]
    is_latent = i * tq < n_lat
    bound = bound_ref[0]

    def over_keys(step):
        @pl.when(is_latent)
        def _():
            def body(c, carry):
                off = pl.multiple_of(c * tk, tk)
                step(k_ref[0, pl.ds(off, tk), :], v_ref[0, pl.ds(off, tk), :])
                return carry
            lax.fori_loop(0, n_lat // tk, body, 0)
        step(k_ref[0, n_lat:t_all, :], v_ref[0, n_lat:t_all, :])

    def write_out(j, o2):
        sl = slice(j * LANES, (j + 1) * LANES)
        o = jnp.where(lo, o2[:tq], o2[tq:])
        o_ref[0, :, sl] = (o * g_ref[0, :, sl].astype(F32)).astype(BF16)

    @pl.when(bound <= GQA_SHIFT_LIMIT)
    def _():
        l_sc[...] = jnp.zeros_like(l_sc)
        acc_sc[...] = jnp.zeros_like(acc_sc)

        def step(k, v):
            width = k.shape[0]
            for j in range(pairs):
                p = jnp.exp2(_dot_t(q2_sc[j], k) - bound)
                part = p[:, :LANES]
                for cidx in range(1, width // LANES):
                    part = part + p[:, cidx * LANES:(cidx + 1) * LANES]
                l_sc[j] = l_sc[j] + part
                acc_sc[j] = acc_sc[j] + jnp.dot(p.astype(BF16), v, preferred_element_type=F32)

        over_keys(step)
        for j in range(pairs):
            write_out(j, acc_sc[j] / jnp.sum(l_sc[j], axis=-1, keepdims=True))

    @pl.when(bound > GQA_SHIFT_LIMIT)
    def _():
        m_sc[...] = jnp.full_like(m_sc, MASKED)
        l_sc[...] = jnp.zeros_like(l_sc)
        acc_sc[...] = jnp.zeros_like(acc_sc)

        def step(k, v):
            width = k.shape[0]
            for j in range(pairs):
                s = _dot_t(q2_sc[j], k)
                m_old = m_sc[j]
                m_new = jnp.maximum(m_old, jnp.max(s, axis=-1, keepdims=True))
                alpha = jnp.exp2(m_old - m_new)
                p = jnp.exp2(s - jnp.concatenate([m_new] * (width // LANES), axis=1))
                l_sc[j] = alpha * l_sc[j] + jnp.sum(p, axis=-1, keepdims=True)
                acc_sc[j] = alpha * acc_sc[j] + jnp.dot(p.astype(BF16), v, preferred_element_type=F32)
                m_sc[j] = m_new

        over_keys(step)
        for j in range(pairs):
            write_out(j, acc_sc[j] / l_sc[j])


def _gqa_score_bound(q_norm_w, k_norm_w):
    bound = HEAD_D * jnp.max(jnp.abs(q_norm_w)) * jnp.max(jnp.abs(k_norm_w)) * (HEAD_D ** -0.5 * LOG2E)
    return (bound * 1.02).reshape(1).astype(F32)


def _gqa_call(bound, gq, gk, gv, gg, n_lat):
    b, t, _ = gq.shape
    tq = ROW_BLOCK
    tk = GQA_KEY_CHUNK
    pairs = GQA_W // LANES
    tok = pl.BlockSpec((1, tq, GQA_W), lambda bi, i: (bi, i, 0))
    kv = pl.BlockSpec((1, t, GQA_KV_W), lambda bi, i: (bi, 0, 0))
    return pl.pallas_call(
        functools.partial(_gqa_kernel, n_lat=n_lat, tk=tk),
        grid=(b, t // tq),
        in_specs=[pl.BlockSpec(memory_space=pltpu.SMEM), tok, kv, kv, tok],
        out_specs=tok,
        out_shape=jax.ShapeDtypeStruct((b, t, GQA_W), BF16),
        scratch_shapes=[pltpu.VMEM((pairs, 2 * tq, LANES), BF16)] + [pltpu.VMEM((pairs, 2 * tq, LANES), F32)] * 3,
        compiler_params=_params(("parallel", "arbitrary")),
        name="gqa_attn",
    )(bound, gq, gk, gv, gg)


def _merge_kernel(x_ref, mod_ref, of_ref, ob_ref, rg_ref, gnw_ref, na_ref, gqa_ref, mg_ref,
                  wr_ref, wn_ref, wg_ref, wo_ref, fw_ref, o_ref, *, final):
    o = of_ref[0] + ob_ref[0]
    parts = []
    for h in range(RET_HEADS):
        oh = o[:, h * RET_D:(h + 1) * RET_D]
        mu = jnp.mean(oh, axis=-1, keepdims=True)
        dev = oh - mu
        var = jnp.mean(dev * dev, axis=-1, keepdims=True)
        parts.append(dev * lax.rsqrt(var + EPS))
    ret_x = (jnp.concatenate(parts, axis=-1) * gnw_ref[...] * rg_ref[0].astype(F32)).astype(BF16)
    m = mg_ref[0, :, :D_MODEL].astype(F32) * jnp.dot(ret_x, wr_ref[...], preferred_element_type=F32)
    m = m + mg_ref[0, :, D_MODEL:2 * D_MODEL].astype(F32) * jnp.dot(na_ref[0], wn_ref[...],
                                                                    preferred_element_type=F32)
    m = m + mg_ref[0, :, 2 * D_MODEL:].astype(F32) * jnp.dot(gqa_ref[0], wg_ref[...],
                                                             preferred_element_type=F32)
    out = jnp.dot(m.astype(BF16), wo_ref[...], preferred_element_type=F32)
    gate = mod_ref[0][:, 2 * D_MODEL:]
    xn = x_ref[0] + gate * out
    if final:
        ms = jnp.mean(xn * xn, axis=-1, keepdims=True)
        xn = xn * lax.rsqrt(ms + EPS) * fw_ref[...]
    o_ref[0] = xn


def _merge_call(xa, mod, ret_o, rg, gnw, na_o, gqa_o, mg, wr, wn, wg, wo, fw, n_lat, final):
    b, t, _ = xa.shape
    tm = ROW_BLOCK
    t_out = n_lat if final else t
    tok = lambda w: pl.BlockSpec((1, tm, w), lambda bi, i: (bi, i, 0))
    const2 = lambda shape: pl.BlockSpec(shape, lambda bi, i: (0, 0))
    return pl.pallas_call(
        functools.partial(_merge_kernel, final=final),
        grid=(b, t_out // tm),
        in_specs=[
            tok(D_MODEL),
            pl.BlockSpec((1, 1, 3 * D_MODEL), _mod_index(n_lat // tm, b)),
            tok(RET_W),
            tok(RET_W),
            tok(RET_W),
            const2((1, RET_W)),
            tok(NA_W),
            tok(GQA_W),
            tok(N_BRANCH * D_MODEL),
            const2((RET_W, D_MODEL)),
            const2((NA_W, D_MODEL)),
            const2((GQA_W, D_MODEL)),
            const2((D_MODEL, D_MODEL)),
            const2((1, D_MODEL)),
        ],
        out_specs=tok(D_MODEL),
        out_shape=jax.ShapeDtypeStruct((b, t_out, D_MODEL), F32),
        compiler_params=_params(("parallel", "parallel")),
        name="merge",
    )(xa, mod, ret_o[0], ret_o[1], rg, gnw, na_o, gqa_o, mg, wr, wn, wg, wo, fw)


def _forward(x, c, ctx, c_ctx, ada_w, ada_b, norm_w, w_in, ret_log_decay, ret_gn_w, na_rpb,
             q_norm_w, k_norm_w, w_ret_o, w_na_o, w_gqa_o, w_out, final_norm_w):
    b, n, _ = x.shape
    n_ctx = ctx.shape[1]
    depth = ada_w.shape[0]
    t = n + n_ctx
    rows = n // GRID_W
    assert n % ROW_BLOCK == 0 and n_ctx == ROW_BLOCK and n % n_ctx == 0 and n % GQA_KEY_CHUNK == 0
    assert rows >= NA_WIN_UNITS * NA_UNIT_ROWS and rows % NA_UNIT_ROWS == 0

    w_in_p = _permute_w_in(w_in).astype(BF16)
    qw = _k_layout(jnp.concatenate([q_norm_w, q_norm_w], axis=-1)).reshape(depth, 1, LANES)
    kw = _k_layout(jnp.concatenate([k_norm_w, k_norm_w], axis=-1)).reshape(depth, 1, LANES)
    w_ret_p = w_ret_o.astype(BF16)
    w_na_p = w_na_o.astype(BF16)
    w_gqa_p = _o_layout(w_gqa_o, axis=-2).astype(BF16)
    w_out_p = w_out.astype(BF16)
    na_bias = _na_bias_tables(na_rpb, rows)

    cos, sin = _rope_tables(n, t)
    mod_rows = 8 * ((b + 1 + 7) // 8)
    c_rows = jnp.concatenate([c, c_ctx[None, :], jnp.zeros((mod_rows - b - 1, D_MODEL), F32)], axis=0)
    mod_all = _ada_call(c_rows, ada_w, ada_b)

    xa = jnp.concatenate([x, ctx], axis=1)
    for layer in range(depth):
        final = layer == depth - 1
        mod = mod_all[layer].reshape(mod_rows, 1, 3 * D_MODEL)
        (rq, rk, rv, rg, nq, nk, nv, ng, gq, gk, gv, gg, mg) = _inproj_call(
            xa, mod, norm_w[layer].reshape(1, D_MODEL), w_in_p[layer], cos, sin, qw[layer], kw[layer], n)
        ret_o = _ret_call(ret_log_decay[layer], rq, rk, rv, n)
        na_o = _na_call(nq, nk, nv, ng, na_bias[layer], n, n_ctx)
        gqa_o = _gqa_call(_gqa_score_bound(q_norm_w[layer], k_norm_w[layer]), gq, gk, gv, gg, n)
        xa = _merge_call(xa, mod, ret_o, rg, ret_gn_w[layer].reshape(1, RET_W), na_o, gqa_o, mg,
                         w_ret_p[layer], w_na_p[layer], w_gqa_p[layer], w_out_p[layer],
                         final_norm_w.reshape(1, D_MODEL), n, final)
    return xa


def kernel(x, c, ctx, c_ctx, ada_w, ada_b, norm_w, w_in, ret_log_decay, ret_gn_w, na_rpb, q_norm_w, k_norm_w,
           w_ret_o, w_na_o, w_gqa_o, w_out, final_norm_w):
    return _forward(x, c, ctx, c_ctx, ada_w, ada_b, norm_w, w_in, ret_log_decay, ret_gn_w, na_rpb,
                    q_norm_w, k_norm_w, w_ret_o, w_na_o, w_gqa_o, w_out, final_norm_w)
```

```python
import functools

import numpy as np
import jax
import jax.numpy as jnp
from jax import lax
from jax.experimental import pallas as pl
from jax.experimental.pallas import tpu as pltpu

F32 = jnp.float32
BF16 = jnp.bfloat16

D_MODEL = 1024
GRID_W = 64
EPS = 1e-6
ROPE_THETA = 10000.0
RET_HEADS = 4
RET_D = 128
RET_W = RET_HEADS * RET_D
RET_STEP = 256
NA_HEADS = 8
HEAD_D = 64
NA_W = NA_HEADS * HEAD_D
NA_WIN_ROWS = 8
NA_WIN_COLS = 16
GQA_HEADS = 8
GQA_KV_HEADS = 2
GQA_W = GQA_HEADS * HEAD_D
GQA_KV_W = GQA_KV_HEADS * HEAD_D
N_BRANCH = 3
LANES = 128
ROW_BLOCK = 256
NA_UNIT_ROWS = ROW_BLOCK // GRID_W
NA_WIN_UNITS = 3
GQA_KEY_CHUNK = 1024
LOG2E = 1.4426950408889634
GQA_SHIFT_LIMIT = 40.0
MASKED = -1e30
VMEM_LIMIT = 56 * 1024 * 1024

_SPLITS = (RET_W, RET_W, RET_W, RET_W, NA_W, NA_W, NA_W, NA_W, GQA_W, GQA_KV_W, GQA_KV_W, GQA_W,
           N_BRANCH * D_MODEL)
_OFFS = np.concatenate([[0], np.cumsum(_SPLITS)])
IN_COLS = int(_OFFS[-1])
(O_RQ, O_RK, O_RV, O_RG, O_NQ, O_NK, O_NV, O_NG, O_GQ, O_GK, O_GV, O_GG, O_MG) = (int(o) for o in _OFFS[:-1])


def _params(sem, vmem=VMEM_LIMIT):
    return pltpu.CompilerParams(dimension_semantics=sem, vmem_limit_bytes=vmem)


def _silu(x):
    return x / (1.0 + jnp.exp(-x))


def _sigmoid(x):
    return 1.0 / (1.0 + jnp.exp(-x))


def _lane_dims():
    l = np.arange(LANES)
    half = l // 64
    slot = (l % 64) // 32
    r = l % 32
    d = np.where(r < 16, r, 32 + r - 16) + 16 * half
    return slot, d


def _transpose_cols(w, shape, order, axis=-1):
    axis = axis % w.ndim
    k = len(shape)
    full = w.shape[:axis] + tuple(shape) + w.shape[axis + 1:]
    perm = tuple(range(axis)) + tuple(axis + o for o in order) + tuple(range(axis + k, len(full)))
    return jnp.transpose(w.reshape(full), perm).reshape(w.shape)


def _q_layout(w):
    return _transpose_cols(w, (2, GQA_HEADS // 2, 2, 2, 16), (1, 3, 0, 2, 4))


def _k_layout(w):
    return _transpose_cols(w, (2, 2, 2, 16), (2, 0, 1, 3))


def _o_layout(w, axis=-1):
    return _transpose_cols(w, (2, GQA_HEADS // 2, HEAD_D), (1, 0, 2), axis=axis)


def _permute_w_in(w_in):
    parts = [w_in[..., :O_GQ], _q_layout(w_in[..., O_GQ:O_GK]), _k_layout(w_in[..., O_GK:O_GV]),
             w_in[..., O_GV:O_GG], _o_layout(w_in[..., O_GG:O_MG]), w_in[..., O_MG:]]
    return jnp.concatenate(parts, axis=-1)


def _rope_tables(n, t_total):
    _, d = _lane_dims()
    quarter = HEAD_D // 4
    freqs = ROPE_THETA ** (-jnp.arange(quarter, dtype=F32) / quarter)
    t = jnp.arange(n)
    pos_r = (t // GRID_W).astype(F32)
    pos_c = (t % GRID_W).astype(F32)
    dd = d % 32
    fi = dd % quarter
    use_col = (d // 32) == 1
    first = dd < quarter
    ang = jnp.where(use_col[None, :], pos_c[:, None], pos_r[:, None]) * freqs[fi][None, :]
    cos = jnp.cos(ang)
    sin = jnp.where(first[None, :], -jnp.sin(ang), jnp.sin(ang))
    pad = t_total - n
    cos = jnp.concatenate([cos, jnp.ones((pad, LANES), F32)], axis=0)
    sin = jnp.concatenate([sin, jnp.zeros((pad, LANES), F32)], axis=0)
    return cos, sin


def _ada_kernel(c_ref, w_ref, b_ref, o_ref):
    cs = _silu(c_ref[...])
    o_ref[0] = jnp.dot(cs, w_ref[0], preferred_element_type=F32) + b_ref[0]


def _ada_call(c_rows, ada_w, ada_b):
    depth = ada_w.shape[0]
    rows = c_rows.shape[0]
    nb = 3
    return pl.pallas_call(
        _ada_kernel,
        grid=(depth, nb),
        in_specs=[
            pl.BlockSpec((rows, D_MODEL), lambda l, j: (0, 0)),
            pl.BlockSpec((1, D_MODEL, D_MODEL), lambda l, j: (l, 0, j)),
            pl.BlockSpec((1, 1, D_MODEL), lambda l, j: (l, 0, j)),
        ],
        out_specs=pl.BlockSpec((1, rows, D_MODEL), lambda l, j: (l, 0, j)),
        out_shape=jax.ShapeDtypeStruct((depth, rows, nb * D_MODEL), F32),
        compiler_params=_params(("parallel", "parallel")),
        name="ada_mod",
    )(c_rows, ada_w, ada_b.reshape(depth, 1, nb * D_MODEL))


def _head_mean_sq(t, mask_a):
    sq = t * t
    s_all = jnp.sum(sq, axis=-1, keepdims=True)
    s_a = jnp.sum(jnp.where(mask_a, sq, 0.0), axis=-1, keepdims=True)
    return jnp.where(mask_a, s_a, s_all - s_a) * (1.0 / HEAD_D)


def _norm_rope(t, w_lane, cos, sin, mask_a):
    tn = t * lax.rsqrt(_head_mean_sq(t, mask_a) + EPS) * w_lane
    return tn * cos + pltpu.roll(tn, LANES // 2, 1) * sin


def _inproj_kernel(x_ref, mod_ref, nw_ref, w_ref, cos_ref, sin_ref, qw_ref, kw_ref,
                   rq_ref, rk_ref, rv_ref, rg_ref, nq_ref, nk_ref, nv_ref, ng_ref,
                   gq_ref, gk_ref, gv_ref, gg_ref, mg_ref):
    x = x_ref[0]
    ms = jnp.mean(x * x, axis=-1, keepdims=True)
    y = x * lax.rsqrt(ms + EPS) * nw_ref[...]
    mod = mod_ref[0]
    shift = mod[:, :D_MODEL]
    scale = mod[:, D_MODEL:2 * D_MODEL]
    h = (y * (1.0 + scale) + shift).astype(BF16)

    def proj(off, width):
        return jnp.dot(h, w_ref[:, off:off + width], preferred_element_type=F32)

    qk_scale = HEAD_D ** -0.5
    rq_ref[0] = proj(O_RQ, RET_W).astype(BF16)
    rk_ref[0] = proj(O_RK, RET_W).astype(BF16)
    rv_ref[0] = proj(O_RV, RET_W).astype(BF16)
    rg_ref[0] = _silu(proj(O_RG, RET_W)).astype(BF16)
    nq_ref[0] = (proj(O_NQ, NA_W) * (qk_scale * LOG2E)).astype(BF16)
    nk_ref[0] = proj(O_NK, NA_W).astype(BF16)
    nv_ref[0] = proj(O_NV, NA_W).astype(BF16)
    ng_ref[0] = _silu(proj(O_NG, NA_W)).astype(BF16)

    lane = lax.broadcasted_iota(jnp.int32, (1, LANES), 1)
    mask_a = (lane % 64) < 32
    cos = cos_ref[...]
    sin = sin_ref[...]
    qw = qw_ref[...]
    for j in range(GQA_W // LANES):
        t = proj(O_GQ + j * LANES, LANES)
        gq_ref[0, :, j * LANES:(j + 1) * LANES] = (_norm_rope(t, qw, cos, sin, mask_a)
                                                   * (qk_scale * LOG2E)).astype(BF16)
    gk_ref[0] = _norm_rope(proj(O_GK, GQA_KV_W), kw_ref[...], cos, sin, mask_a).astype(BF16)
    gv_ref[0] = proj(O_GV, GQA_KV_W).astype(BF16)
    gg_ref[0] = _silu(proj(O_GG, GQA_W)).astype(BF16)
    for j in range(N_BRANCH):
        mg_ref[0, :, j * D_MODEL:(j + 1) * D_MODEL] = _sigmoid(proj(O_MG + j * D_MODEL, D_MODEL)).astype(BF16)


def _mod_index(n_lat_blocks, batch):
    return lambda b, i: (jnp.where(i < n_lat_blocks, b, batch), 0, 0)


def _inproj_call(xa, mod, norm_w, w_in, cos, sin, qw, kw, n_lat):
    b, t, _ = xa.shape
    tm = ROW_BLOCK
    widths = (RET_W, RET_W, RET_W, RET_W, NA_W, NA_W, NA_W, NA_W, GQA_W, GQA_KV_W, GQA_KV_W, GQA_W,
              N_BRANCH * D_MODEL)
    tok = lambda w: pl.BlockSpec((1, tm, w), lambda bi, i: (bi, i, 0))
    const2 = lambda shape: pl.BlockSpec(shape, lambda bi, i: (0, 0))
    return pl.pallas_call(
        _inproj_kernel,
        grid=(b, t // tm),
        in_specs=[
            tok(D_MODEL),
            pl.BlockSpec((1, 1, 3 * D_MODEL), _mod_index(n_lat // tm, b)),
            const2((1, D_MODEL)),
            pl.BlockSpec((D_MODEL, IN_COLS), lambda bi, i: (0, 0), pipeline_mode=pl.Buffered(1)),
            pl.BlockSpec((tm, LANES), lambda bi, i: (i, 0)),
            pl.BlockSpec((tm, LANES), lambda bi, i: (i, 0)),
            const2((1, LANES)),
            const2((1, LANES)),
        ],
        out_specs=[tok(w) for w in widths],
        out_shape=[jax.ShapeDtypeStruct((b, t, w), BF16) for w in widths],
        compiler_params=_params(("parallel", "parallel")),
        name="in_proj",
    )(xa, mod, norm_w, w_in, cos, sin, qw, kw)


def _ret_kernel(rld_ref, qf_ref, kf_ref, vf_ref, qb_ref, kb_ref, vb_ref, of_ref, ob_ref,
                state, dec, xi, zeta, gch):
    s = pl.program_id(1)
    c = RET_STEP
    ksc = RET_D ** -0.5

    @pl.when(s == 0)
    def _():
        state[...] = jnp.zeros_like(state)
        ii = lax.broadcasted_iota(jnp.int32, (c, c), 0)
        jj = lax.broadcasted_iota(jnp.int32, (c, c), 1)
        ri = lax.broadcasted_iota(jnp.int32, (c, RET_D), 0)
        for d in range(2):
            delta = ((ii - jj) if d == 0 else (jj - ii)).astype(F32)
            ip = (ri if d == 0 else c - 1 - ri).astype(F32)
            for h in range(RET_HEADS):
                log_g = -jnp.exp(jnp.full((1, 1), rld_ref[d, h], F32))
                dec[d, h] = jnp.where(delta >= 0, jnp.exp(jnp.maximum(delta, 0.0) * log_g), 0.0) * ksc
                xi[d, h] = jnp.exp((ip + 1.0) * log_g)
                zeta[d, h] = jnp.exp((c - 1.0 - ip) * log_g) * ksc
                gch[d, h] = jnp.exp(jnp.full((RET_D, RET_D), float(c), F32) * log_g)

    for d, (q_ref, k_ref, v_ref, o_ref) in enumerate(((qf_ref, kf_ref, vf_ref, of_ref),
                                                      (qb_ref, kb_ref, vb_ref, ob_ref))):
        for h in range(RET_HEADS):
            sl = slice(h * RET_D, (h + 1) * RET_D)
            q = q_ref[0, :, sl]
            k = k_ref[0, :, sl]
            v = v_ref[0, :, sl]
            scores = lax.dot_general(q, k, (((1,), (1,)), ((), ())), preferred_element_type=F32)
            intra = jnp.dot((scores * dec[d, h]).astype(BF16), v, preferred_element_type=F32)
            st = state[d, h]
            inter = jnp.dot(q, st.astype(BF16), preferred_element_type=F32) * xi[d, h]
            o_ref[0, :, sl] = intra + inter
            vz = (v.astype(F32) * zeta[d, h]).astype(BF16)
            upd = lax.dot_general(k, vz, (((0,), (0,)), ((), ())), preferred_element_type=F32)
            state[d, h] = gch[d, h] * st + upd


def _ret_call(rld, rq, rk, rv, n_lat):
    b, t, _ = rq.shape
    c = RET_STEP
    nc = t // c
    ncn = n_lat // c
    fwd = pl.BlockSpec((1, c, RET_W), lambda bi, s: (bi, (s + ncn) % nc, 0))
    bwd = pl.BlockSpec((1, c, RET_W), lambda bi, s: (bi, nc - 1 - s, 0))
    per_head = lambda rows, cols: pltpu.VMEM((2, RET_HEADS, rows, cols), F32)
    out = jax.ShapeDtypeStruct((b, t, RET_W), F32)
    return pl.pallas_call(
        _ret_kernel,
        grid=(b, nc),
        in_specs=[pl.BlockSpec(memory_space=pltpu.SMEM), fwd, fwd, fwd, bwd, bwd, bwd],
        out_specs=[fwd, bwd],
        out_shape=[out, out],
        scratch_shapes=[per_head(RET_D, RET_D), per_head(c, c), per_head(c, RET_D), per_head(c, RET_D),
                        per_head(RET_D, RET_D)],
        compiler_params=_params(("parallel", "arbitrary")),
        name="retention",
    )(rld, rq, rk, rv, rq, rk, rv)


def _pair_split(q, mask_a):
    zero = jnp.zeros_like(q)
    return jnp.concatenate([jnp.where(mask_a, q, zero), jnp.where(mask_a, zero, q)], axis=0)


def _dot_t(a, b):
    return lax.dot_general(a, b, (((1,), (1,)), ((), ())), preferred_element_type=F32)


def _na_kernel(idx_ref, q_ref, k0_ref, k1_ref, k2_ref, v0_ref, v1_ref, v2_ref, kc_ref, vc_ref, bias_ref,
               g_ref, o_ref, *, g_units):
    m = q_ref.shape[1]
    u = ROW_BLOCK
    n_ctx = kc_ref.shape[1]
    g = pl.program_id(1)
    case = jnp.where(g == 0, 0, jnp.where(g < g_units - 1, 1, jnp.where(g == g_units - 1, 2, 3)))
    tile_of = [[idx_ref[case, rho, kp] for kp in range(NA_WIN_UNITS * NA_UNIT_ROWS // 2)]
               for rho in range(NA_UNIT_ROWS)]
    lane = lax.broadcasted_iota(jnp.int32, (1, LANES), 1)
    lo = lane < 64
    for j in range(NA_W // LANES):
        sl = slice(j * LANES, (j + 1) * LANES)
        q2 = _pair_split(q_ref[0, :, sl], lo)

        def bias(i):
            return jnp.concatenate(
                [jnp.concatenate([bias_ref[j, x, tile_of[rho][2 * i]], bias_ref[j, x, tile_of[rho][2 * i + 1]]], axis=1)
                 for x in range(2) for rho in range(NA_UNIT_ROWS)], axis=0)

        s = jnp.concatenate(
            [_dot_t(q2, kc_ref[0, :, sl])]
            + [_dot_t(q2, kr[0, :, sl]) + bias(i) for i, kr in enumerate((k0_ref, k1_ref, k2_ref))],
            axis=1)
        p = jnp.exp2(s - jnp.max(s, axis=-1, keepdims=True))
        den = jnp.sum(p, axis=-1, keepdims=True)
        pb = p.astype(BF16)
        acc = jnp.dot(pb[:, :n_ctx], vc_ref[0, :, sl], preferred_element_type=F32)
        for i, vr in enumerate((v0_ref, v1_ref, v2_ref)):
            acc = acc + jnp.dot(pb[:, n_ctx + i * u:n_ctx + (i + 1) * u], vr[0, :, sl],
                                preferred_element_type=F32)
        o2 = acc / den
        o = jnp.where(lo, o2[:m], o2[m:])
        o_ref[0, :, sl] = (o * g_ref[0, :, sl].astype(F32)).astype(BF16)


def _na_bias_tables(rpb, rows):
    ur = NA_UNIT_ROWS
    g_units = rows // ur
    n_dr = 2 * NA_WIN_ROWS - 1
    n_dc = 2 * NA_WIN_COLS - 1
    qc = np.arange(GRID_W)
    cstart = np.clip(qc - NA_WIN_COLS // 2, 0, GRID_W - NA_WIN_COLS)
    col_ok = (qc[None, :] >= cstart[:, None]) & (qc[None, :] < cstart[:, None] + NA_WIN_COLS)
    dc = qc[None, :] - qc[:, None] + NA_WIN_COLS - 1
    hot_c = ((dc[None] == np.arange(n_dc)[:, None, None]) & col_ok[None]).astype(np.float32)
    hi = lax.Precision.HIGHEST
    toep = jnp.einsum('lhrd,dqj->lhrqj', rpb.astype(F32) * LOG2E, jnp.asarray(hot_c), precision=hi)
    toep = jnp.where(jnp.asarray(col_ok), toep, MASKED)
    toep = jnp.concatenate([toep, jnp.full_like(toep[:, :, :1], MASKED)], axis=2)

    combos = {(n_dr, n_dr): 0}
    index = np.zeros((4, ur, NA_WIN_UNITS * ur // 2), np.int32)
    for case, g in enumerate((0, 1, g_units - 1)):
        ws = int(np.clip(g - 1, 0, g_units - NA_WIN_UNITS)) * ur
        for rho in range(ur):
            r = g * ur + rho
            rs = int(np.clip(r - NA_WIN_ROWS // 2, 0, rows - NA_WIN_ROWS))
            for kp in range(NA_WIN_UNITS * ur // 2):
                pair = []
                for kr in (ws + 2 * kp, ws + 2 * kp + 1):
                    ok = rs <= kr < rs + NA_WIN_ROWS
                    pair.append(kr - r + NA_WIN_ROWS - 1 if ok else n_dr)
                index[case, rho, kp] = combos.setdefault(tuple(pair), len(combos))
    left = np.array([c[0] for c in combos], np.int32)
    right = np.array([c[1] for c in combos], np.int32)
    tiles = jnp.concatenate([jnp.take(toep, left, axis=2), jnp.take(toep, right, axis=2)], axis=-1)
    depth = rpb.shape[0]
    tiles = tiles.reshape(depth, NA_HEADS // 2, 2, len(combos), GRID_W, 2 * GRID_W)
    return tiles, jnp.asarray(index)


def _na_call(nq, nk, nv, ng, tiles, index, n_lat, n_ctx):
    b, t, _ = nq.shape
    u = ROW_BLOCK
    g_units = n_lat // u
    n_units = t // u

    def win(i):
        return lambda bi, g: (bi, jnp.clip(g - 1, 0, g_units - NA_WIN_UNITS) + i, 0)

    tok = pl.BlockSpec((1, u, NA_W), lambda bi, g: (bi, g, 0))
    ctx = pl.BlockSpec((1, n_ctx, NA_W), lambda bi, g: (bi, n_lat // n_ctx, 0))
    kv = [pl.BlockSpec((1, u, NA_W), win(i)) for i in range(NA_WIN_UNITS)]
    return pl.pallas_call(
        functools.partial(_na_kernel, g_units=g_units),
        grid=(b, n_units),
        in_specs=[pl.BlockSpec(memory_space=pltpu.SMEM), tok] + kv + kv
        + [ctx, ctx, pl.BlockSpec(tiles.shape, lambda bi, g: (0,) * tiles.ndim), tok],
        out_specs=tok,
        out_shape=jax.ShapeDtypeStruct((b, t, NA_W), BF16),
        compiler_params=_params(("parallel", "arbitrary")),
        name="nbr_attn",
    )(index, nq, nk, nk, nk, nv, nv, nv, nk, nv, tiles, ng)


def _gqa_kernel(bound_ref, q_ref, k_ref, v_ref, g_ref, o_ref, q2_sc, m_sc, l_sc, acc_sc, *, n_lat, tk):
    i = pl.program_id(1)
    tq = q_ref.shape[1]
    pairs = GQA_W // LANES
    lane = lax.broadcasted_iota(jnp.int32, (1, LANES), 1)
    mask_a = (lane % 64) < 32
    lo = lane < 64
    for j in range(pairs):
        q2_sc[j] = _pair_split(q_ref[0, :, j * LANES:(j + 1) * LANES], mask_a)
    t_all = k_ref.shape[1]
    is_latent = i * tq < n_lat
    bound = bound_ref[0]

    def over_keys(step):
        @pl.when(is_latent)
        def _():
            def body(c, carry):
                off = pl.multiple_of(c * tk, tk)
                step(k_ref[0, pl.ds(off, tk), :], v_ref[0, pl.ds(off, tk), :])
                return carry
            lax.fori_loop(0, n_lat // tk, body, 0)
        step(k_ref[0, n_lat:t_all, :], v_ref[0, n_lat:t_all, :])

    def write_out(j, o2):
        sl = slice(j * LANES, (j + 1) * LANES)
        o = jnp.where(lo, o2[:tq], o2[tq:])
        o_ref[0, :, sl] = (o * g_ref[0, :, sl].astype(F32)).astype(BF16)

    @pl.when(bound <= GQA_SHIFT_LIMIT)
    def _():
        l_sc[...] = jnp.zeros_like(l_sc)
        acc_sc[...] = jnp.zeros_like(acc_sc)

        def step(k, v):
            width = k.shape[0]
            for j in range(pairs):
                p = jnp.exp2(_dot_t(q2_sc[j], k) - bound)
                part = p[:, :LANES]
                for cidx in range(1, width // LANES):
                    part = part + p[:, cidx * LANES:(cidx + 1) * LANES]
                l_sc[j] = l_sc[j] + part
                acc_sc[j] = acc_sc[j] + jnp.dot(p.astype(BF16), v, preferred_element_type=F32)

        over_keys(step)
        for j in range(pairs):
            write_out(j, acc_sc[j] / jnp.sum(l_sc[j], axis=-1, keepdims=True))

    @pl.when(bound > GQA_SHIFT_LIMIT)
    def _():
        m_sc[...] = jnp.full_like(m_sc, MASKED)
        l_sc[...] = jnp.zeros_like(l_sc)
        acc_sc[...] = jnp.zeros_like(acc_sc)

        def step(k, v):
            width = k.shape[0]
            for j in range(pairs):
                s = _dot_t(q2_sc[j], k)
                m_old = m_sc[j]
                m_new = jnp.maximum(m_old, jnp.max(s, axis=-1, keepdims=True))
                alpha = jnp.exp2(m_old - m_new)
                p = jnp.exp2(s - jnp.concatenate([m_new] * (width // LANES), axis=1))
                l_sc[j] = alpha * l_sc[j] + jnp.sum(p, axis=-1, keepdims=True)
                acc_sc[j] = alpha * acc_sc[j] + jnp.dot(p.astype(BF16), v, preferred_element_type=F32)
                m_sc[j] = m_new

        over_keys(step)
        for j in range(pairs):
            write_out(j, acc_sc[j] / l_sc[j])


def _gqa_score_bound(q_norm_w, k_norm_w):
    bound = HEAD_D * jnp.max(jnp.abs(q_norm_w)) * jnp.max(jnp.abs(k_norm_w)) * (HEAD_D ** -0.5 * LOG2E)
    return (bound * 1.02).reshape(1).astype(F32)


def _gqa_call(bound, gq, gk, gv, gg, n_lat):
    b, t, _ = gq.shape
    tq = ROW_BLOCK
    tk = GQA_KEY_CHUNK
    pairs = GQA_W // LANES
    tok = pl.BlockSpec((1, tq, GQA_W), lambda bi, i: (bi, i, 0))
    kv = pl.BlockSpec((1, t, GQA_KV_W), lambda bi, i: (bi, 0, 0))
    return pl.pallas_call(
        functools.partial(_gqa_kernel, n_lat=n_lat, tk=tk),
        grid=(b, t // tq),
        in_specs=[pl.BlockSpec(memory_space=pltpu.SMEM), tok, kv, kv, tok],
        out_specs=tok,
        out_shape=jax.ShapeDtypeStruct((b, t, GQA_W), BF16),
        scratch_shapes=[pltpu.VMEM((pairs, 2 * tq, LANES), BF16)] + [pltpu.VMEM((pairs, 2 * tq, LANES), F32)] * 3,
        compiler_params=_params(("parallel", "arbitrary")),
        name="gqa_attn",
    )(bound, gq, gk, gv, gg)


def _merge_kernel(x_ref, mod_ref, of_ref, ob_ref, rg_ref, gnw_ref, na_ref, gqa_ref, mg_ref,
                  wr_ref, wn_ref, wg_ref, wo_ref, fw_ref, o_ref, *, final):
    o = of_ref[0] + ob_ref[0]
    parts = []
    for h in range(RET_HEADS):
        oh = o[:, h * RET_D:(h + 1) * RET_D]
        mu = jnp.mean(oh, axis=-1, keepdims=True)
        dev = oh - mu
        var = jnp.mean(dev * dev, axis=-1, keepdims=True)
        parts.append(dev * lax.rsqrt(var + EPS))
    ret_x = (jnp.concatenate(parts, axis=-1) * gnw_ref[...] * rg_ref[0].astype(F32)).astype(BF16)
    m = mg_ref[0, :, :D_MODEL].astype(F32) * jnp.dot(ret_x, wr_ref[...], preferred_element_type=F32)
    m = m + mg_ref[0, :, D_MODEL:2 * D_MODEL].astype(F32) * jnp.dot(na_ref[0], wn_ref[...],
                                                                    preferred_element_type=F32)
    m = m + mg_ref[0, :, 2 * D_MODEL:].astype(F32) * jnp.dot(gqa_ref[0], wg_ref[...],
                                                             preferred_element_type=F32)
    out = jnp.dot(m.astype(BF16), wo_ref[...], preferred_element_type=F32)
    gate = mod_ref[0][:, 2 * D_MODEL:]
    xn = x_ref[0] + gate * out
    if final:
        ms = jnp.mean(xn * xn, axis=-1, keepdims=True)
        xn = xn * lax.rsqrt(ms + EPS) * fw_ref[...]
    o_ref[0] = xn


def _merge_call(xa, mod, ret_o, rg, gnw, na_o, gqa_o, mg, wr, wn, wg, wo, fw, n_lat, final):
    b, t, _ = xa.shape
    tm = ROW_BLOCK
    t_out = n_lat if final else t
    tok = lambda w: pl.BlockSpec((1, tm, w), lambda bi, i: (bi, i, 0))
    const2 = lambda shape: pl.BlockSpec(shape, lambda bi, i: (0, 0))
    return pl.pallas_call(
        functools.partial(_merge_kernel, final=final),
        grid=(b, t_out // tm),
        in_specs=[
            tok(D_MODEL),
            pl.BlockSpec((1, 1, 3 * D_MODEL), _mod_index(n_lat // tm, b)),
            tok(RET_W),
            tok(RET_W),
            tok(RET_W),
            const2((1, RET_W)),
            tok(NA_W),
            tok(GQA_W),
            tok(N_BRANCH * D_MODEL),
            const2((RET_W, D_MODEL)),
            const2((NA_W, D_MODEL)),
            const2((GQA_W, D_MODEL)),
            const2((D_MODEL, D_MODEL)),
            const2((1, D_MODEL)),
        ],
        out_specs=tok(D_MODEL),
        out_shape=jax.ShapeDtypeStruct((b, t_out, D_MODEL), F32),
        compiler_params=_params(("parallel", "parallel")),
        name="merge",
    )(xa, mod, ret_o[0], ret_o[1], rg, gnw, na_o, gqa_o, mg, wr, wn, wg, wo, fw)


def _forward(x, c, ctx, c_ctx, ada_w, ada_b, norm_w, w_in, ret_log_decay, ret_gn_w, na_rpb,
             q_norm_w, k_norm_w, w_ret_o, w_na_o, w_gqa_o, w_out, final_norm_w):
    b, n, _ = x.shape
    n_ctx = ctx.shape[1]
    depth = ada_w.shape[0]
    t = n + n_ctx
    rows = n // GRID_W
    assert n % ROW_BLOCK == 0 and n_ctx == ROW_BLOCK and n % n_ctx == 0 and n % GQA_KEY_CHUNK == 0
    assert rows >= NA_WIN_UNITS * NA_UNIT_ROWS and rows % NA_UNIT_ROWS == 0

    w_in_p = _permute_w_in(w_in).astype(BF16)
    qw = _k_layout(jnp.concatenate([q_norm_w, q_norm_w], axis=-1)).reshape(depth, 1, LANES)
    kw = _k_layout(jnp.concatenate([k_norm_w, k_norm_w], axis=-1)).reshape(depth, 1, LANES)
    w_ret_p = w_ret_o.astype(BF16)
    w_na_p = w_na_o.astype(BF16)
    w_gqa_p = _o_layout(w_gqa_o, axis=-2).astype(BF16)
    w_out_p = w_out.astype(BF16)
    na_tiles, na_index = _na_bias_tables(na_rpb, rows)

    cos, sin = _rope_tables(n, t)
    mod_rows = 8 * ((b + 1 + 7) // 8)
    c_rows = jnp.concatenate([c, c_ctx[None, :], jnp.zeros((mod_rows - b - 1, D_MODEL), F32)], axis=0)
    mod_all = _ada_call(c_rows, ada_w, ada_b)

    xa = jnp.concatenate([x, ctx], axis=1)
    for layer in range(depth):
        final = layer == depth - 1
        mod = mod_all[layer].reshape(mod_rows, 1, 3 * D_MODEL)
        (rq, rk, rv, rg, nq, nk, nv, ng, gq, gk, gv, gg, mg) = _inproj_call(
            xa, mod, norm_w[layer].reshape(1, D_MODEL), w_in_p[layer], cos, sin, qw[layer], kw[layer], n)
        ret_o = _ret_call(ret_log_decay[layer], rq, rk, rv, n)
        na_o = _na_call(nq, nk, nv, ng, na_tiles[layer], na_index, n, n_ctx)
        gqa_o = _gqa_call(_gqa_score_bound(q_norm_w[layer], k_norm_w[layer]), gq, gk, gv, gg, n)
        xa = _merge_call(xa, mod, ret_o, rg, ret_gn_w[layer].reshape(1, RET_W), na_o, gqa_o, mg,
                         w_ret_p[layer], w_na_p[layer], w_gqa_p[layer], w_out_p[layer],
                         final_norm_w.reshape(1, D_MODEL), n, final)
    return xa


def kernel(x, c, ctx, c_ctx, ada_w, ada_b, norm_w, w_in, ret_log_decay, ret_gn_w, na_rpb, q_norm_w, k_norm_w,
           w_ret_o, w_na_o, w_gqa_o, w_out, final_norm_w):
    return _forward(x, c, ctx, c_ctx, ada_w, ada_b, norm_w, w_in, ret_log_decay, ret_gn_w, na_rpb,
                    q_norm_w, k_norm_w, w_ret_o, w_na_o, w_gqa_o, w_out, final_norm_w)
```

```python
import functools

import numpy as np
import jax
import jax.numpy as jnp
from jax import lax
from jax.experimental import pallas as pl
from jax.experimental.pallas import tpu as pltpu

F32 = jnp.float32
BF16 = jnp.bfloat16

D_MODEL = 1024
GRID_W = 64
EPS = 1e-6
ROPE_THETA = 10000.0
RET_HEADS = 4
RET_D = 128
RET_W = RET_HEADS * RET_D
RET_STEP = 256
NA_HEADS = 8
HEAD_D = 64
NA_W = NA_HEADS * HEAD_D
NA_WIN_ROWS = 8
NA_WIN_COLS = 16
GQA_HEADS = 8
GQA_KV_HEADS = 2
GQA_W = GQA_HEADS * HEAD_D
GQA_KV_W = GQA_KV_HEADS * HEAD_D
N_BRANCH = 3
LANES = 128
MXU_COLS = 256
ROW_BLOCK = 256
NA_UNIT_ROWS = ROW_BLOCK // GRID_W
NA_WIN_UNITS = 3
GQA_KEY_CHUNK = 1024
LOG2E = 1.4426950408889634
GQA_SHIFT_LIMIT = 40.0
MASKED = -1e30
VMEM_LIMIT = 56 * 1024 * 1024

_SPLITS = (RET_W, RET_W, RET_W, RET_W, NA_W, NA_W, NA_W, NA_W, GQA_W, GQA_KV_W, GQA_KV_W, GQA_W,
           N_BRANCH * D_MODEL)
_OFFS = np.concatenate([[0], np.cumsum(_SPLITS)])
IN_COLS = int(_OFFS[-1])
(O_RQ, O_RK, O_RV, O_RG, O_NQ, O_NK, O_NV, O_NG, O_GQ, O_GK, O_GV, O_GG, O_MG) = (int(o) for o in _OFFS[:-1])


def _params(sem, vmem=VMEM_LIMIT):
    return pltpu.CompilerParams(dimension_semantics=sem, vmem_limit_bytes=vmem)


def _silu(x):
    return x / (1.0 + jnp.exp(-x))


def _sigmoid(x):
    return 1.0 / (1.0 + jnp.exp(-x))


def _lane_dims():
    l = np.arange(LANES)
    half = l // 64
    slot = (l % 64) // 32
    r = l % 32
    d = np.where(r < 16, r, 32 + r - 16) + 16 * half
    return slot, d


def _transpose_cols(w, shape, order, axis=-1):
    axis = axis % w.ndim
    k = len(shape)
    full = w.shape[:axis] + tuple(shape) + w.shape[axis + 1:]
    perm = tuple(range(axis)) + tuple(axis + o for o in order) + tuple(range(axis + k, len(full)))
    return jnp.transpose(w.reshape(full), perm).reshape(w.shape)


def _q_layout(w):
    return _transpose_cols(w, (2, GQA_HEADS // 2, 2, 2, 16), (1, 3, 0, 2, 4))


def _k_layout(w):
    return _transpose_cols(w, (2, 2, 2, 16), (2, 0, 1, 3))


def _o_layout(w, axis=-1):
    return _transpose_cols(w, (2, GQA_HEADS // 2, HEAD_D), (1, 0, 2), axis=axis)


def _permute_w_in(w_in):
    parts = [w_in[..., :O_GQ], _q_layout(w_in[..., O_GQ:O_GK]), _k_layout(w_in[..., O_GK:O_GV]),
             w_in[..., O_GV:O_GG], _o_layout(w_in[..., O_GG:O_MG]), w_in[..., O_MG:]]
    return jnp.concatenate(parts, axis=-1)


def _rope_tables(n, t_total):
    _, d = _lane_dims()
    quarter = HEAD_D // 4
    freqs = ROPE_THETA ** (-jnp.arange(quarter, dtype=F32) / quarter)
    t = jnp.arange(n)
    pos_r = (t // GRID_W).astype(F32)
    pos_c = (t % GRID_W).astype(F32)
    dd = d % 32
    fi = dd % quarter
    use_col = (d // 32) == 1
    first = dd < quarter
    ang = jnp.where(use_col[None, :], pos_c[:, None], pos_r[:, None]) * freqs[fi][None, :]
    cos = jnp.cos(ang)
    sin = jnp.where(first[None, :], -jnp.sin(ang), jnp.sin(ang))
    pad = t_total - n
    cos = jnp.concatenate([cos, jnp.ones((pad, LANES), F32)], axis=0)
    sin = jnp.concatenate([sin, jnp.zeros((pad, LANES), F32)], axis=0)
    return cos, sin


def _ada_kernel(c_ref, w_ref, b_ref, o_ref):
    cs = _silu(c_ref[...])
    o_ref[0] = jnp.dot(cs, w_ref[0], preferred_element_type=F32) + b_ref[0]


def _ada_call(c_rows, ada_w, ada_b):
    depth = ada_w.shape[0]
    rows = c_rows.shape[0]
    nb = 3
    return pl.pallas_call(
        _ada_kernel,
        grid=(depth, nb),
        in_specs=[
            pl.BlockSpec((rows, D_MODEL), lambda l, j: (0, 0)),
            pl.BlockSpec((1, D_MODEL, D_MODEL), lambda l, j: (l, 0, j)),
            pl.BlockSpec((1, 1, D_MODEL), lambda l, j: (l, 0, j)),
        ],
        out_specs=pl.BlockSpec((1, rows, D_MODEL), lambda l, j: (l, 0, j)),
        out_shape=jax.ShapeDtypeStruct((depth, rows, nb * D_MODEL), F32),
        compiler_params=_params(("parallel", "parallel")),
        name="ada_mod",
    )(c_rows, ada_w, ada_b.reshape(depth, 1, nb * D_MODEL))


def _head_mean_sq(t, mask_a):
    sq = t * t
    s_all = jnp.sum(sq, axis=-1, keepdims=True)
    s_a = jnp.sum(jnp.where(mask_a, sq, 0.0), axis=-1, keepdims=True)
    return jnp.where(mask_a, s_a, s_all - s_a) * (1.0 / HEAD_D)


def _norm_rope(t, w_lane, cos, sin, mask_a):
    tn = t * lax.rsqrt(_head_mean_sq(t, mask_a) + EPS) * w_lane
    return tn * cos + pltpu.roll(tn, LANES // 2, 1) * sin


def _inproj_kernel(x_ref, mod_ref, nw_ref, w_ref, cos_ref, sin_ref, qw_ref, kw_ref,
                   rq_ref, rk_ref, rv_ref, rg_ref, nq_ref, nk_ref, nv_ref, ng_ref,
                   gq_ref, gk_ref, gv_ref, gg_ref, mg_ref):
    x = x_ref[0]
    ms = jnp.mean(x * x, axis=-1, keepdims=True)
    y = x * lax.rsqrt(ms + EPS) * nw_ref[...]
    mod = mod_ref[0]
    shift = mod[:, :D_MODEL]
    scale = mod[:, D_MODEL:2 * D_MODEL]
    h = (y * (1.0 + scale) + shift).astype(BF16)

    def proj(off, width):
        return jnp.dot(h, w_ref[:, off:off + width], preferred_element_type=F32)

    qk_scale = HEAD_D ** -0.5
    rq_ref[0] = proj(O_RQ, RET_W).astype(BF16)
    rk_ref[0] = proj(O_RK, RET_W).astype(BF16)
    rv_ref[0] = proj(O_RV, RET_W).astype(BF16)
    rg_ref[0] = _silu(proj(O_RG, RET_W)).astype(BF16)
    nq_ref[0] = (proj(O_NQ, NA_W) * (qk_scale * LOG2E)).astype(BF16)
    nk_ref[0] = proj(O_NK, NA_W).astype(BF16)
    nv_ref[0] = proj(O_NV, NA_W).astype(BF16)
    ng_ref[0] = _silu(proj(O_NG, NA_W)).astype(BF16)

    lane = lax.broadcasted_iota(jnp.int32, (1, LANES), 1)
    mask_a = (lane % 64) < 32
    cos = cos_ref[...]
    sin = sin_ref[...]
    qw = qw_ref[...]
    for j2 in range(GQA_W // MXU_COLS):
        t2 = proj(O_GQ + j2 * MXU_COLS, MXU_COLS)
        for j in range(2 * j2, 2 * j2 + 2):
            t = t2[:, (j % 2) * LANES:(j % 2 + 1) * LANES]
            gq_ref[0, :, j * LANES:(j + 1) * LANES] = (_norm_rope(t, qw, cos, sin, mask_a)
                                                       * (qk_scale * LOG2E)).astype(BF16)
    kv = proj(O_GK, 2 * GQA_KV_W)
    gk_ref[0] = _norm_rope(kv[:, :GQA_KV_W], kw_ref[...], cos, sin, mask_a).astype(BF16)
    gv_ref[0] = kv[:, GQA_KV_W:].astype(BF16)
    gg_ref[0] = _silu(proj(O_GG, GQA_W)).astype(BF16)
    for j in range(N_BRANCH):
        mg_ref[0, :, j * D_MODEL:(j + 1) * D_MODEL] = _sigmoid(proj(O_MG + j * D_MODEL, D_MODEL)).astype(BF16)


def _mod_index(n_lat_blocks, batch):
    return lambda b, i: (jnp.where(i < n_lat_blocks, b, batch), 0, 0)


def _inproj_call(xa, mod, norm_w, w_in, cos, sin, qw, kw, n_lat):
    b, t, _ = xa.shape
    tm = ROW_BLOCK
    widths = (RET_W, RET_W, RET_W, RET_W, NA_W, NA_W, NA_W, NA_W, GQA_W, GQA_KV_W, GQA_KV_W, GQA_W,
              N_BRANCH * D_MODEL)
    tok = lambda w: pl.BlockSpec((1, tm, w), lambda bi, i: (bi, i, 0))
    const2 = lambda shape: pl.BlockSpec(shape, lambda bi, i: (0, 0))
    return pl.pallas_call(
        _inproj_kernel,
        grid=(b, t // tm),
        in_specs=[
            tok(D_MODEL),
            pl.BlockSpec((1, 1, 3 * D_MODEL), _mod_index(n_lat // tm, b)),
            const2((1, D_MODEL)),
            pl.BlockSpec((D_MODEL, IN_COLS), lambda bi, i: (0, 0), pipeline_mode=pl.Buffered(1)),
            pl.BlockSpec((tm, LANES), lambda bi, i: (i, 0)),
            pl.BlockSpec((tm, LANES), lambda bi, i: (i, 0)),
            const2((1, LANES)),
            const2((1, LANES)),
        ],
        out_specs=[tok(w) for w in widths],
        out_shape=[jax.ShapeDtypeStruct((b, t, w), BF16) for w in widths],
        compiler_params=_params(("parallel", "parallel")),
        name="in_proj",
    )(xa, mod, norm_w, w_in, cos, sin, qw, kw)


def _ret_kernel(rld_ref, qf_ref, kf_ref, vf_ref, qb_ref, kb_ref, vb_ref, of_ref, ob_ref,
                state, dec, xi, zeta, gch):
    s = pl.program_id(1)
    c = RET_STEP
    ksc = RET_D ** -0.5

    @pl.when(s == 0)
    def _():
        state[...] = jnp.zeros_like(state)
        ii = lax.broadcasted_iota(jnp.int32, (c, c), 0)
        jj = lax.broadcasted_iota(jnp.int32, (c, c), 1)
        ri = lax.broadcasted_iota(jnp.int32, (c, RET_D), 0)
        for d in range(2):
            delta = ((ii - jj) if d == 0 else (jj - ii)).astype(F32)
            ip = (ri if d == 0 else c - 1 - ri).astype(F32)
            for h in range(RET_HEADS):
                log_g = -jnp.exp(jnp.full((1, 1), rld_ref[d, h], F32))
                dec[d, h] = jnp.where(delta >= 0, jnp.exp(jnp.maximum(delta, 0.0) * log_g), 0.0) * ksc
                xi[d, h] = jnp.exp((ip + 1.0) * log_g)
                zeta[d, h] = jnp.exp((c - 1.0 - ip) * log_g) * ksc
                gch[d, h] = jnp.exp(jnp.full((RET_D, RET_D), float(c), F32) * log_g)

    for d, (q_ref, k_ref, v_ref, o_ref) in enumerate(((qf_ref, kf_ref, vf_ref, of_ref),
                                                      (qb_ref, kb_ref, vb_ref, ob_ref))):
        for h in range(RET_HEADS):
            sl = slice(h * RET_D, (h + 1) * RET_D)
            q = q_ref[0, :, sl]
            k = k_ref[0, :, sl]
            v = v_ref[0, :, sl]
            scores = lax.dot_general(q, k, (((1,), (1,)), ((), ())), preferred_element_type=F32)
            intra = jnp.dot((scores * dec[d, h]).astype(BF16), v, preferred_element_type=F32)
            st = state[d, h]
            inter = jnp.dot(q, st.astype(BF16), preferred_element_type=F32) * xi[d, h]
            o_ref[0, :, sl] = intra + inter
            vz = (v.astype(F32) * zeta[d, h]).astype(BF16)
            upd = lax.dot_general(k, vz, (((0,), (0,)), ((), ())), preferred_element_type=F32)
            state[d, h] = gch[d, h] * st + upd


def _ret_call(rld, rq, rk, rv, n_lat):
    b, t, _ = rq.shape
    c = RET_STEP
    nc = t // c
    ncn = n_lat // c
    fwd = pl.BlockSpec((1, c, RET_W), lambda bi, s: (bi, (s + ncn) % nc, 0))
    bwd = pl.BlockSpec((1, c, RET_W), lambda bi, s: (bi, nc - 1 - s, 0))
    per_head = lambda rows, cols: pltpu.VMEM((2, RET_HEADS, rows, cols), F32)
    out = jax.ShapeDtypeStruct((b, t, RET_W), F32)
    return pl.pallas_call(
        _ret_kernel,
        grid=(b, nc),
        in_specs=[pl.BlockSpec(memory_space=pltpu.SMEM), fwd, fwd, fwd, bwd, bwd, bwd],
        out_specs=[fwd, bwd],
        out_shape=[out, out],
        scratch_shapes=[per_head(RET_D, RET_D), per_head(c, c), per_head(c, RET_D), per_head(c, RET_D),
                        per_head(RET_D, RET_D)],
        compiler_params=_params(("parallel", "arbitrary")),
        name="retention",
    )(rld, rq, rk, rv, rq, rk, rv)


def _pair_split(q, mask_a):
    zero = jnp.zeros_like(q)
    return jnp.concatenate([jnp.where(mask_a, q, zero), jnp.where(mask_a, zero, q)], axis=0)


def _dot_t(a, b):
    return lax.dot_general(a, b, (((1,), (1,)), ((), ())), preferred_element_type=F32)


def _na_kernel(idx_ref, q_ref, k0_ref, k1_ref, k2_ref, v0_ref, v1_ref, v2_ref, kc_ref, vc_ref, bias_ref,
               g_ref, o_ref, *, g_units):
    m = q_ref.shape[1]
    u = ROW_BLOCK
    n_ctx = kc_ref.shape[1]
    g = pl.program_id(1)
    case = jnp.where(g == 0, 0, jnp.where(g < g_units - 1, 1, jnp.where(g == g_units - 1, 2, 3)))
    tile_of = [[idx_ref[case, rho, kp] for kp in range(NA_WIN_UNITS * NA_UNIT_ROWS // 2)]
               for rho in range(NA_UNIT_ROWS)]
    lane = lax.broadcasted_iota(jnp.int32, (1, LANES), 1)
    lo = lane < 64
    pairs = NA_W // LANES
    slices = [slice(j * LANES, (j + 1) * LANES) for j in range(pairs)]
    raw = [_dot_t(_pair_split(q_ref[0, :, sl], lo),
                  jnp.concatenate([r[0, :, sl] for r in (kc_ref, k0_ref, k1_ref, k2_ref)], axis=0))
           for sl in slices]
    for j, sl in enumerate(slices):

        def bias(i):
            return jnp.concatenate(
                [jnp.concatenate([bias_ref[j, x, tile_of[rho][2 * i]], bias_ref[j, x, tile_of[rho][2 * i + 1]]], axis=1)
                 for x in range(2) for rho in range(NA_UNIT_ROWS)], axis=0)

        v_all = jnp.concatenate([r[0, :, sl] for r in (vc_ref, v0_ref, v1_ref, v2_ref)], axis=0)
        s = raw[j]
        s = jnp.concatenate([s[:, :n_ctx]] + [s[:, n_ctx + i * u:n_ctx + (i + 1) * u] + bias(i)
                                              for i in range(NA_WIN_UNITS)], axis=1)
        p = jnp.exp2(s - jnp.max(s, axis=-1, keepdims=True))
        den = jnp.sum(p, axis=-1, keepdims=True)
        acc = jnp.dot(p.astype(BF16), v_all, preferred_element_type=F32)
        o2 = acc / den
        o = jnp.where(lo, o2[:m], o2[m:])
        o_ref[0, :, sl] = (o * g_ref[0, :, sl].astype(F32)).astype(BF16)


def _na_bias_tables(rpb, rows):
    ur = NA_UNIT_ROWS
    g_units = rows // ur
    n_dr = 2 * NA_WIN_ROWS - 1
    n_dc = 2 * NA_WIN_COLS - 1
    qc = np.arange(GRID_W)
    cstart = np.clip(qc - NA_WIN_COLS // 2, 0, GRID_W - NA_WIN_COLS)
    col_ok = (qc[None, :] >= cstart[:, None]) & (qc[None, :] < cstart[:, None] + NA_WIN_COLS)
    dc = qc[None, :] - qc[:, None] + NA_WIN_COLS - 1
    hot_c = ((dc[None] == np.arange(n_dc)[:, None, None]) & col_ok[None]).astype(np.float32)
    hi = lax.Precision.HIGHEST
    toep = jnp.einsum('lhrd,dqj->lhrqj', rpb.astype(F32) * LOG2E, jnp.asarray(hot_c), precision=hi)
    toep = jnp.where(jnp.asarray(col_ok), toep, MASKED)
    toep = jnp.concatenate([toep, jnp.full_like(toep[:, :, :1], MASKED)], axis=2)

    combos = {(n_dr, n_dr): 0}
    index = np.zeros((4, ur, NA_WIN_UNITS * ur // 2), np.int32)
    for case, g in enumerate((0, 1, g_units - 1)):
        ws = int(np.clip(g - 1, 0, g_units - NA_WIN_UNITS)) * ur
        for rho in range(ur):
            r = g * ur + rho
            rs = int(np.clip(r - NA_WIN_ROWS // 2, 0, rows - NA_WIN_ROWS))
            for kp in range(NA_WIN_UNITS * ur // 2):
                pair = []
                for kr in (ws + 2 * kp, ws + 2 * kp + 1):
                    ok = rs <= kr < rs + NA_WIN_ROWS
                    pair.append(kr - r + NA_WIN_ROWS - 1 if ok else n_dr)
                index[case, rho, kp] = combos.setdefault(tuple(pair), len(combos))
    left = np.array([c[0] for c in combos], np.int32)
    right = np.array([c[1] for c in combos], np.int32)
    tiles = jnp.concatenate([jnp.take(toep, left, axis=2), jnp.take(toep, right, axis=2)], axis=-1)
    depth = rpb.shape[0]
    tiles = tiles.reshape(depth, NA_HEADS // 2, 2, len(combos), GRID_W, 2 * GRID_W)
    return tiles, jnp.asarray(index)


def _na_call(nq, nk, nv, ng, tiles, index, n_lat, n_ctx):
    b, t, _ = nq.shape
    u = ROW_BLOCK
    g_units = n_lat // u
    n_units = t // u

    def win(i):
        return lambda bi, g: (bi, jnp.clip(g - 1, 0, g_units - NA_WIN_UNITS) + i, 0)

    tok = pl.BlockSpec((1, u, NA_W), lambda bi, g: (bi, g, 0))
    ctx = pl.BlockSpec((1, n_ctx, NA_W), lambda bi, g: (bi, n_lat // n_ctx, 0))
    kv = [pl.BlockSpec((1, u, NA_W), win(i)) for i in range(NA_WIN_UNITS)]
    return pl.pallas_call(
        functools.partial(_na_kernel, g_units=g_units),
        grid=(b, n_units),
        in_specs=[pl.BlockSpec(memory_space=pltpu.SMEM), tok] + kv + kv
        + [ctx, ctx, pl.BlockSpec(tiles.shape, lambda bi, g: (0,) * tiles.ndim), tok],
        out_specs=tok,
        out_shape=jax.ShapeDtypeStruct((b, t, NA_W), BF16),
        compiler_params=_params(("parallel", "arbitrary")),
        name="nbr_attn",
    )(index, nq, nk, nk, nk, nv, nv, nv, nk, nv, tiles, ng)


def _gqa_kernel(bound_ref, q_ref, k_ref, v_ref, g_ref, o_ref, q2_sc, m_sc, l_sc, acc_sc, *, n_lat, tk):
    i = pl.program_id(1)
    tq = q_ref.shape[1]
    pairs = GQA_W // LANES
    lane = lax.broadcasted_iota(jnp.int32, (1, LANES), 1)
    mask_a = (lane % 64) < 32
    lo = lane < 64
    for j in range(pairs):
        q2_sc[j] = _pair_split(q_ref[0, :, j * LANES:(j + 1) * LANES], mask_a)
    t_all = k_ref.shape[1]
    is_latent = i * tq < n_lat
    bound = bound_ref[0]

    def over_keys(step):
        @pl.when(is_latent)
        def _():
            def body(c, carry):
                off = pl.multiple_of(c * tk, tk)
                step(k_ref[0, pl.ds(off, tk), :], v_ref[0, pl.ds(off, tk), :])
                return carry
            lax.fori_loop(0, n_lat // tk, body, 0)
        step(k_ref[0, n_lat:t_all, :], v_ref[0, n_lat:t_all, :])

    def write_out(j, o2):
        sl = slice(j * LANES, (j + 1) * LANES)
        o = jnp.where(lo, o2[:tq], o2[tq:])
        o_ref[0, :, sl] = (o * g_ref[0, :, sl].astype(F32)).astype(BF16)

    @pl.when(bound <= GQA_SHIFT_LIMIT)
    def _():
        l_sc[...] = jnp.zeros_like(l_sc)
        acc_sc[...] = jnp.zeros_like(acc_sc)

        def step(k, v):
            width = k.shape[0]
            for j in range(pairs):
                p = jnp.exp2(_dot_t(q2_sc[j], k) - bound)
                part = p[:, :LANES]
                for cidx in range(1, width // LANES):
                    part = part + p[:, cidx * LANES:(cidx + 1) * LANES]
                l_sc[j] = l_sc[j] + part
                acc_sc[j] = acc_sc[j] + jnp.dot(p.astype(BF16), v, preferred_element_type=F32)

        over_keys(step)
        for j in range(pairs):
            write_out(j, acc_sc[j] / jnp.sum(l_sc[j], axis=-1, keepdims=True))

    @pl.when(bound > GQA_SHIFT_LIMIT)
    def _():
        m_sc[...] = jnp.full_like(m_sc, MASKED)
        l_sc[...] = jnp.zeros_like(l_sc)
        acc_sc[...] = jnp.zeros_like(acc_sc)

        def step(k, v):
            width = k.shape[0]
            for j in range(pairs):
                s = _dot_t(q2_sc[j], k)
                m_old = m_sc[j]
                m_new = jnp.maximum(m_old, jnp.max(s, axis=-1, keepdims=True))
                alpha = jnp.exp2(m_old - m_new)
                p = jnp.exp2(s - jnp.concatenate([m_new] * (width // LANES), axis=1))
                l_sc[j] = alpha * l_sc[j] + jnp.sum(p, axis=-1, keepdims=True)
                acc_sc[j] = alpha * acc_sc[j] + jnp.dot(p.astype(BF16), v, preferred_element_type=F32)
                m_sc[j] = m_new

        over_keys(step)
        for j in range(pairs):
            write_out(j, acc_sc[j] / l_sc[j])


def _gqa_score_bound(q_norm_w, k_norm_w):
    bound = HEAD_D * jnp.max(jnp.abs(q_norm_w)) * jnp.max(jnp.abs(k_norm_w)) * (HEAD_D ** -0.5 * LOG2E)
    return (bound * 1.02).reshape(1).astype(F32)


def _gqa_call(bound, gq, gk, gv, gg, n_lat):
    b, t, _ = gq.shape
    tq = ROW_BLOCK
    tk = GQA_KEY_CHUNK
    pairs = GQA_W // LANES
    tok = pl.BlockSpec((1, tq, GQA_W), lambda bi, i: (bi, i, 0))
    kv = pl.BlockSpec((1, t, GQA_KV_W), lambda bi, i: (bi, 0, 0))
    return pl.pallas_call(
        functools.partial(_gqa_kernel, n_lat=n_lat, tk=tk),
        grid=(b, t // tq),
        in_specs=[pl.BlockSpec(memory_space=pltpu.SMEM), tok, kv, kv, tok],
        out_specs=tok,
        out_shape=jax.ShapeDtypeStruct((b, t, GQA_W), BF16),
        scratch_shapes=[pltpu.VMEM((pairs, 2 * tq, LANES), BF16)] + [pltpu.VMEM((pairs, 2 * tq, LANES), F32)] * 3,
        compiler_params=_params(("parallel", "arbitrary")),
        name="gqa_attn",
    )(bound, gq, gk, gv, gg)


def _merge_kernel(x_ref, mod_ref, of_ref, ob_ref, rg_ref, gnw_ref, na_ref, gqa_ref, mg_ref,
                  wr_ref, wn_ref, wg_ref, wo_ref, fw_ref, o_ref, *, final):
    o = of_ref[0] + ob_ref[0]
    parts = []
    for h in range(RET_HEADS):
        oh = o[:, h * RET_D:(h + 1) * RET_D]
        mu = jnp.mean(oh, axis=-1, keepdims=True)
        dev = oh - mu
        var = jnp.mean(dev * dev, axis=-1, keepdims=True)
        parts.append(dev * lax.rsqrt(var + EPS))
    ret_x = (jnp.concatenate(parts, axis=-1) * gnw_ref[...] * rg_ref[0].astype(F32)).astype(BF16)
    m = mg_ref[0, :, :D_MODEL].astype(F32) * jnp.dot(ret_x, wr_ref[...], preferred_element_type=F32)
    m = m + mg_ref[0, :, D_MODEL:2 * D_MODEL].astype(F32) * jnp.dot(na_ref[0], wn_ref[...],
                                                                    preferred_element_type=F32)
    m = m + mg_ref[0, :, 2 * D_MODEL:].astype(F32) * jnp.dot(gqa_ref[0], wg_ref[...],
                                                             preferred_element_type=F32)
    out = jnp.dot(m.astype(BF16), wo_ref[...], preferred_element_type=F32)
    gate = mod_ref[0][:, 2 * D_MODEL:]
    xn = x_ref[0] + gate * out
    if final:
        ms = jnp.mean(xn * xn, axis=-1, keepdims=True)
        xn = xn * lax.rsqrt(ms + EPS) * fw_ref[...]
    o_ref[0] = xn


def _merge_call(xa, mod, ret_o, rg, gnw, na_o, gqa_o, mg, wr, wn, wg, wo, fw, n_lat, final):
    b, t, _ = xa.shape
    tm = ROW_BLOCK
    t_out = n_lat if final else t
    tok = lambda w: pl.BlockSpec((1, tm, w), lambda bi, i: (bi, i, 0))
    const2 = lambda shape: pl.BlockSpec(shape, lambda bi, i: (0, 0))
    return pl.pallas_call(
        functools.partial(_merge_kernel, final=final),
        grid=(b, t_out // tm),
        in_specs=[
            tok(D_MODEL),
            pl.BlockSpec((1, 1, 3 * D_MODEL), _mod_index(n_lat // tm, b)),
            tok(RET_W),
            tok(RET_W),
            tok(RET_W),
            const2((1, RET_W)),
            tok(NA_W),
            tok(GQA_W),
            tok(N_BRANCH * D_MODEL),
            const2((RET_W, D_MODEL)),
            const2((NA_W, D_MODEL)),
            const2((GQA_W, D_MODEL)),
            const2((D_MODEL, D_MODEL)),
            const2((1, D_MODEL)),
        ],
        out_specs=tok(D_MODEL),
        out_shape=jax.ShapeDtypeStruct((b, t_out, D_MODEL), F32),
        compiler_params=_params(("parallel", "parallel")),
        name="merge",
    )(xa, mod, ret_o[0], ret_o[1], rg, gnw, na_o, gqa_o, mg, wr, wn, wg, wo, fw)


def _forward(x, c, ctx, c_ctx, ada_w, ada_b, norm_w, w_in, ret_log_decay, ret_gn_w, na_rpb,
             q_norm_w, k_norm_w, w_ret_o, w_na_o, w_gqa_o, w_out, final_norm_w):
    b, n, _ = x.shape
    n_ctx = ctx.shape[1]
    depth = ada_w.shape[0]
    t = n + n_ctx
    rows = n // GRID_W
    assert n % ROW_BLOCK == 0 and n_ctx == ROW_BLOCK and n % n_ctx == 0 and n % GQA_KEY_CHUNK == 0
    assert rows >= NA_WIN_UNITS * NA_UNIT_ROWS and rows % NA_UNIT_ROWS == 0

    w_in_p = _permute_w_in(w_in).astype(BF16)
    qw = _k_layout(jnp.concatenate([q_norm_w, q_norm_w], axis=-1)).reshape(depth, 1, LANES)
    kw = _k_layout(jnp.concatenate([k_norm_w, k_norm_w], axis=-1)).reshape(depth, 1, LANES)
    w_ret_p = w_ret_o.astype(BF16)
    w_na_p = w_na_o.astype(BF16)
    w_gqa_p = _o_layout(w_gqa_o, axis=-2).astype(BF16)
    w_out_p = w_out.astype(BF16)
    na_tiles, na_index = _na_bias_tables(na_rpb, rows)

    cos, sin = _rope_tables(n, t)
    mod_rows = 8 * ((b + 1 + 7) // 8)
    c_rows = jnp.concatenate([c, c_ctx[None, :], jnp.zeros((mod_rows - b - 1, D_MODEL), F32)], axis=0)
    mod_all = _ada_call(c_rows, ada_w, ada_b)

    xa = jnp.concatenate([x, ctx], axis=1)
    for layer in range(depth):
        final = layer == depth - 1
        mod = mod_all[layer].reshape(mod_rows, 1, 3 * D_MODEL)
        (rq, rk, rv, rg, nq, nk, nv, ng, gq, gk, gv, gg, mg) = _inproj_call(
            xa, mod, norm_w[layer].reshape(1, D_MODEL), w_in_p[layer], cos, sin, qw[layer], kw[layer], n)
        ret_o = _ret_call(ret_log_decay[layer], rq, rk, rv, n)
        na_o = _na_call(nq, nk, nv, ng, na_tiles[layer], na_index, n, n_ctx)
        gqa_o = _gqa_call(_gqa_score_bound(q_norm_w[layer], k_norm_w[layer]), gq, gk, gv, gg, n)
        xa = _merge_call(xa, mod, ret_o, rg, ret_gn_w[layer].reshape(1, RET_W), na_o, gqa_o, mg,
                         w_ret_p[layer], w_na_p[layer], w_gqa_p[layer], w_out_p[layer],
                         final_norm_w.reshape(1, D_MODEL), n, final)
    return xa


def kernel(x, c, ctx, c_ctx, ada_w, ada_b, norm_w, w_in, ret_log_decay, ret_gn_w, na_rpb, q_norm_w, k_norm_w,
           w_ret_o, w_na_o, w_gqa_o, w_out, final_norm_w):
    return _forward(x, c, ctx, c_ctx, ada_w, ada_b, norm_w, w_in, ret_log_decay, ret_gn_w, na_rpb,
                    q_norm_w, k_norm_w, w_ret_o, w_na_o, w_gqa_o, w_out, final_norm_w)
```

```python
import functools

import numpy as np
import jax
import jax.numpy as jnp
from jax import lax
from jax.experimental import pallas as pl
from jax.experimental.pallas import tpu as pltpu

F32 = jnp.float32
BF16 = jnp.bfloat16

D_MODEL = 1024
GRID_W = 64
EPS = 1e-6
ROPE_THETA = 10000.0
RET_HEADS = 4
RET_D = 128
RET_W = RET_HEADS * RET_D
RET_STEP = 256
NA_HEADS = 8
HEAD_D = 64
NA_W = NA_HEADS * HEAD_D
NA_WIN_ROWS = 8
NA_WIN_COLS = 16
GQA_HEADS = 8
GQA_KV_HEADS = 2
GQA_W = GQA_HEADS * HEAD_D
GQA_KV_W = GQA_KV_HEADS * HEAD_D
N_BRANCH = 3
LANES = 128
MXU_COLS = 256
ROW_BLOCK = 256
NA_UNIT_ROWS = ROW_BLOCK // GRID_W
NA_WIN_UNITS = 3
GQA_KEY_CHUNK = 4096
GQA_ONLINE_KEY_CHUNK = 2048
LOG2E = 1.4426950408889634
GQA_SHIFT_LIMIT = 40.0
MASKED = -1e30
VMEM_LIMIT = 56 * 1024 * 1024

_SPLITS = (RET_W, RET_W, RET_W, RET_W, NA_W, NA_W, NA_W, NA_W, GQA_W, GQA_KV_W, GQA_KV_W, GQA_W,
           N_BRANCH * D_MODEL)
_OFFS = np.concatenate([[0], np.cumsum(_SPLITS)])
IN_COLS = int(_OFFS[-1])
(O_RQ, O_RK, O_RV, O_RG, O_NQ, O_NK, O_NV, O_NG, O_GQ, O_GK, O_GV, O_GG, O_MG) = (int(o) for o in _OFFS[:-1])


def _params(sem, vmem=VMEM_LIMIT):
    return pltpu.CompilerParams(dimension_semantics=sem, vmem_limit_bytes=vmem)


def _silu(x):
    return x / (1.0 + jnp.exp(-x))


def _sigmoid(x):
    return 1.0 / (1.0 + jnp.exp(-x))


def _lane_dims():
    l = np.arange(LANES)
    half = l // 64
    slot = (l % 64) // 32
    r = l % 32
    d = np.where(r < 16, r, 32 + r - 16) + 16 * half
    return slot, d


def _transpose_cols(w, shape, order, axis=-1):
    axis = axis % w.ndim
    k = len(shape)
    full = w.shape[:axis] + tuple(shape) + w.shape[axis + 1:]
    perm = tuple(range(axis)) + tuple(axis + o for o in order) + tuple(range(axis + k, len(full)))
    return jnp.transpose(w.reshape(full), perm).reshape(w.shape)


def _q_layout(w):
    return _transpose_cols(w, (2, GQA_HEADS // 2, 2, 2, 16), (1, 3, 0, 2, 4))


def _k_layout(w):
    return _transpose_cols(w, (2, 2, 2, 16), (2, 0, 1, 3))


def _o_layout(w, axis=-1):
    return _transpose_cols(w, (2, GQA_HEADS // 2, HEAD_D), (1, 0, 2), axis=axis)


def _permute_w_in(w_in):
    parts = [w_in[..., :O_GQ], _q_layout(w_in[..., O_GQ:O_GK]), _k_layout(w_in[..., O_GK:O_GV]),
             w_in[..., O_GV:O_GG], _o_layout(w_in[..., O_GG:O_MG]), w_in[..., O_MG:]]
    return jnp.concatenate(parts, axis=-1)


def _rope_tables(n, t_total):
    _, d = _lane_dims()
    quarter = HEAD_D // 4
    freqs = ROPE_THETA ** (-jnp.arange(quarter, dtype=F32) / quarter)
    t = jnp.arange(n)
    pos_r = (t // GRID_W).astype(F32)
    pos_c = (t % GRID_W).astype(F32)
    dd = d % 32
    fi = dd % quarter
    use_col = (d // 32) == 1
    first = dd < quarter
    ang = jnp.where(use_col[None, :], pos_c[:, None], pos_r[:, None]) * freqs[fi][None, :]
    cos = jnp.cos(ang)
    sin = jnp.where(first[None, :], -jnp.sin(ang), jnp.sin(ang))
    pad = t_total - n
    cos = jnp.concatenate([cos, jnp.ones((pad, LANES), F32)], axis=0)
    sin = jnp.concatenate([sin, jnp.zeros((pad, LANES), F32)], axis=0)
    return cos, sin


def _ada_kernel(c_ref, w_ref, b_ref, o_ref):
    cs = _silu(c_ref[...])
    o_ref[0] = jnp.dot(cs, w_ref[0], preferred_element_type=F32) + b_ref[0]


def _ada_call(c_rows, ada_w, ada_b):
    depth = ada_w.shape[0]
    rows = c_rows.shape[0]
    nb = 3
    return pl.pallas_call(
        _ada_kernel,
        grid=(depth, nb),
        in_specs=[
            pl.BlockSpec((rows, D_MODEL), lambda l, j: (0, 0)),
            pl.BlockSpec((1, D_MODEL, D_MODEL), lambda l, j: (l, 0, j)),
            pl.BlockSpec((1, 1, D_MODEL), lambda l, j: (l, 0, j)),
        ],
        out_specs=pl.BlockSpec((1, rows, D_MODEL), lambda l, j: (l, 0, j)),
        out_shape=jax.ShapeDtypeStruct((depth, rows, nb * D_MODEL), F32),
        compiler_params=_params(("parallel", "parallel")),
        name="ada_mod",
    )(c_rows, ada_w, ada_b.reshape(depth, 1, nb * D_MODEL))


def _head_mean_sq(t, mask_a):
    sq = t * t
    s_all = jnp.sum(sq, axis=-1, keepdims=True)
    s_a = jnp.sum(jnp.where(mask_a, sq, 0.0), axis=-1, keepdims=True)
    return jnp.where(mask_a, s_a, s_all - s_a) * (1.0 / HEAD_D)


def _norm_rope(t, w_lane, cos, sin, mask_a):
    tn = t * lax.rsqrt(_head_mean_sq(t, mask_a) + EPS) * w_lane
    return tn * cos + pltpu.roll(tn, LANES // 2, 1) * sin


def _inproj_kernel(x_ref, mod_ref, nw_ref, w_ref, cos_ref, sin_ref, qw_ref, kw_ref,
                   rq_ref, rk_ref, rv_ref, rg_ref, nq_ref, nk_ref, nv_ref, ng_ref,
                   gq_ref, gk_ref, gv_ref, gg_ref, mg_ref):
    x = x_ref[0]
    ms = jnp.mean(x * x, axis=-1, keepdims=True)
    y = x * lax.rsqrt(ms + EPS) * nw_ref[...]
    mod = mod_ref[0]
    shift = mod[:, :D_MODEL]
    scale = mod[:, D_MODEL:2 * D_MODEL]
    h = (y * (1.0 + scale) + shift).astype(BF16)

    def proj(off, width):
        return jnp.dot(h, w_ref[:, off:off + width], preferred_element_type=F32)

    qk_scale = HEAD_D ** -0.5
    rq_ref[0] = proj(O_RQ, RET_W).astype(BF16)
    rk_ref[0] = proj(O_RK, RET_W).astype(BF16)
    rv_ref[0] = proj(O_RV, RET_W).astype(BF16)
    rg_ref[0] = _silu(proj(O_RG, RET_W)).astype(BF16)
    nq_ref[0] = (proj(O_NQ, NA_W) * (qk_scale * LOG2E)).astype(BF16)
    nk_ref[0] = proj(O_NK, NA_W).astype(BF16)
    nv_ref[0] = proj(O_NV, NA_W).astype(BF16)
    ng_ref[0] = _silu(proj(O_NG, NA_W)).astype(BF16)

    lane = lax.broadcasted_iota(jnp.int32, (1, LANES), 1)
    mask_a = (lane % 64) < 32
    cos = cos_ref[...]
    sin = sin_ref[...]
    qw = qw_ref[...]
    for j2 in range(GQA_W // MXU_COLS):
        t2 = proj(O_GQ + j2 * MXU_COLS, MXU_COLS)
        for j in range(2 * j2, 2 * j2 + 2):
            t = t2[:, (j % 2) * LANES:(j % 2 + 1) * LANES]
            gq_ref[0, :, j * LANES:(j + 1) * LANES] = (_norm_rope(t, qw, cos, sin, mask_a)
                                                       * (qk_scale * LOG2E)).astype(BF16)
    kv = proj(O_GK, 2 * GQA_KV_W)
    gk_ref[0] = _norm_rope(kv[:, :GQA_KV_W], kw_ref[...], cos, sin, mask_a).astype(BF16)
    gv_ref[0] = kv[:, GQA_KV_W:].astype(BF16)
    gg_ref[0] = _silu(proj(O_GG, GQA_W)).astype(BF16)
    for j in range(N_BRANCH):
        mg_ref[0, :, j * D_MODEL:(j + 1) * D_MODEL] = _sigmoid(proj(O_MG + j * D_MODEL, D_MODEL)).astype(BF16)


def _mod_index(n_lat_blocks, batch):
    return lambda b, i: (jnp.where(i < n_lat_blocks, b, batch), 0, 0)


def _inproj_call(xa, mod, norm_w, w_in, cos, sin, qw, kw, n_lat):
    b, t, _ = xa.shape
    tm = ROW_BLOCK
    widths = (RET_W, RET_W, RET_W, RET_W, NA_W, NA_W, NA_W, NA_W, GQA_W, GQA_KV_W, GQA_KV_W, GQA_W,
              N_BRANCH * D_MODEL)
    tok = lambda w: pl.BlockSpec((1, tm, w), lambda bi, i: (bi, i, 0))
    const2 = lambda shape: pl.BlockSpec(shape, lambda bi, i: (0, 0))
    return pl.pallas_call(
        _inproj_kernel,
        grid=(b, t // tm),
        in_specs=[
            tok(D_MODEL),
            pl.BlockSpec((1, 1, 3 * D_MODEL), _mod_index(n_lat // tm, b)),
            const2((1, D_MODEL)),
            pl.BlockSpec((D_MODEL, IN_COLS), lambda bi, i: (0, 0), pipeline_mode=pl.Buffered(1)),
            pl.BlockSpec((tm, LANES), lambda bi, i: (i, 0)),
            pl.BlockSpec((tm, LANES), lambda bi, i: (i, 0)),
            const2((1, LANES)),
            const2((1, LANES)),
        ],
        out_specs=[tok(w) for w in widths],
        out_shape=[jax.ShapeDtypeStruct((b, t, w), BF16) for w in widths],
        compiler_params=_params(("parallel", "parallel")),
        name="in_proj",
    )(xa, mod, norm_w, w_in, cos, sin, qw, kw)


def _ret_kernel(rld_ref, qf_ref, kf_ref, vf_ref, qb_ref, kb_ref, vb_ref, of_ref, ob_ref,
                state, dec, xi, zeta, gch):
    s = pl.program_id(1)
    c = RET_STEP
    ksc = RET_D ** -0.5

    @pl.when(s == 0)
    def _():
        state[...] = jnp.zeros_like(state)
        ii = lax.broadcasted_iota(jnp.int32, (c, c), 0)
        jj = lax.broadcasted_iota(jnp.int32, (c, c), 1)
        ri = lax.broadcasted_iota(jnp.int32, (c, RET_D), 0)
        for d in range(2):
            delta = ((ii - jj) if d == 0 else (jj - ii)).astype(F32)
            ip = (ri if d == 0 else c - 1 - ri).astype(F32)
            for h in range(RET_HEADS):
                log_g = -jnp.exp(jnp.full((1, 1), rld_ref[d, h], F32))
                dec[d, h] = jnp.where(delta >= 0, jnp.exp(jnp.maximum(delta, 0.0) * log_g), 0.0) * ksc
                xi[d, h] = jnp.exp((ip + 1.0) * log_g)
                zeta[d, h] = jnp.exp((c - 1.0 - ip) * log_g) * ksc
                gch[d, h] = jnp.exp(jnp.full((RET_D, RET_D), float(c), F32) * log_g)

    for d, (q_ref, k_ref, v_ref, o_ref) in enumerate(((qf_ref, kf_ref, vf_ref, of_ref),
                                                      (qb_ref, kb_ref, vb_ref, ob_ref))):
        for h in range(RET_HEADS):
            sl = slice(h * RET_D, (h + 1) * RET_D)
            q = q_ref[0, :, sl]
            k = k_ref[0, :, sl]
            v = v_ref[0, :, sl]
            scores = lax.dot_general(q, k, (((1,), (1,)), ((), ())), preferred_element_type=F32)
            intra = jnp.dot((scores * dec[d, h]).astype(BF16), v, preferred_element_type=F32)
            st = state[d, h]
            inter = jnp.dot(q, st.astype(BF16), preferred_element_type=F32) * xi[d, h]
            o_ref[0, :, sl] = intra + inter
            vz = (v.astype(F32) * zeta[d, h]).astype(BF16)
            upd = lax.dot_general(k, vz, (((0,), (0,)), ((), ())), preferred_element_type=F32)
            state[d, h] = gch[d, h] * st + upd


def _ret_call(rld, rq, rk, rv, n_lat):
    b, t, _ = rq.shape
    c = RET_STEP
    nc = t // c
    ncn = n_lat // c
    fwd = pl.BlockSpec((1, c, RET_W), lambda bi, s: (bi, (s + ncn) % nc, 0))
    bwd = pl.BlockSpec((1, c, RET_W), lambda bi, s: (bi, nc - 1 - s, 0))
    per_head = lambda rows, cols: pltpu.VMEM((2, RET_HEADS, rows, cols), F32)
    out = jax.ShapeDtypeStruct((b, t, RET_W), F32)
    return pl.pallas_call(
        _ret_kernel,
        grid=(b, nc),
        in_specs=[pl.BlockSpec(memory_space=pltpu.SMEM), fwd, fwd, fwd, bwd, bwd, bwd],
        out_specs=[fwd, bwd],
        out_shape=[out, out],
        scratch_shapes=[per_head(RET_D, RET_D), per_head(c, c), per_head(c, RET_D), per_head(c, RET_D),
                        per_head(RET_D, RET_D)],
        compiler_params=_params(("parallel", "arbitrary")),
        name="retention",
    )(rld, rq, rk, rv, rq, rk, rv)


def _pair_split(q, mask_a):
    zero = jnp.zeros_like(q)
    return jnp.concatenate([jnp.where(mask_a, q, zero), jnp.where(mask_a, zero, q)], axis=0)


def _dot_t(a, b):
    return lax.dot_general(a, b, (((1,), (1,)), ((), ())), preferred_element_type=F32)


def _na_kernel(idx_ref, q_ref, k0_ref, k1_ref, k2_ref, v0_ref, v1_ref, v2_ref, kc_ref, vc_ref, bias_ref,
               g_ref, o_ref, *, g_units):
    m = q_ref.shape[1]
    u = ROW_BLOCK
    n_ctx = kc_ref.shape[1]
    g = pl.program_id(1)
    case = jnp.where(g == 0, 0, jnp.where(g < g_units - 1, 1, jnp.where(g == g_units - 1, 2, 3)))
    tile_of = [[idx_ref[case, rho, kp] for kp in range(NA_WIN_UNITS * NA_UNIT_ROWS // 2)]
               for rho in range(NA_UNIT_ROWS)]
    lane = lax.broadcasted_iota(jnp.int32, (1, LANES), 1)
    lo = lane < 64
    pairs = NA_W // LANES
    slices = [slice(j * LANES, (j + 1) * LANES) for j in range(pairs)]
    raw = [_dot_t(_pair_split(q_ref[0, :, sl], lo),
                  jnp.concatenate([r[0, :, sl] for r in (kc_ref, k0_ref, k1_ref, k2_ref)], axis=0))
           for sl in slices]
    for j, sl in enumerate(slices):

        def bias(i):
            return jnp.concatenate(
                [jnp.concatenate([bias_ref[j, x, tile_of[rho][2 * i]], bias_ref[j, x, tile_of[rho][2 * i + 1]]], axis=1)
                 for x in range(2) for rho in range(NA_UNIT_ROWS)], axis=0)

        v_all = jnp.concatenate([r[0, :, sl] for r in (vc_ref, v0_ref, v1_ref, v2_ref)], axis=0)
        s = raw[j]
        s = jnp.concatenate([s[:, :n_ctx]] + [s[:, n_ctx + i * u:n_ctx + (i + 1) * u] + bias(i)
                                              for i in range(NA_WIN_UNITS)], axis=1)
        p = jnp.exp2(s - jnp.max(s, axis=-1, keepdims=True))
        den = jnp.sum(p, axis=-1, keepdims=True)
        acc = jnp.dot(p.astype(BF16), v_all, preferred_element_type=F32)
        o2 = acc / den
        o = jnp.where(lo, o2[:m], o2[m:])
        o_ref[0, :, sl] = (o * g_ref[0, :, sl].astype(F32)).astype(BF16)


def _na_bias_tables(rpb, rows):
    ur = NA_UNIT_ROWS
    g_units = rows // ur
    n_dr = 2 * NA_WIN_ROWS - 1
    n_dc = 2 * NA_WIN_COLS - 1
    qc = np.arange(GRID_W)
    cstart = np.clip(qc - NA_WIN_COLS // 2, 0, GRID_W - NA_WIN_COLS)
    col_ok = (qc[None, :] >= cstart[:, None]) & (qc[None, :] < cstart[:, None] + NA_WIN_COLS)
    dc = qc[None, :] - qc[:, None] + NA_WIN_COLS - 1
    hot_c = ((dc[None] == np.arange(n_dc)[:, None, None]) & col_ok[None]).astype(np.float32)
    hi = lax.Precision.HIGHEST
    toep = jnp.einsum('lhrd,dqj->lhrqj', rpb.astype(F32) * LOG2E, jnp.asarray(hot_c), precision=hi)
    toep = jnp.where(jnp.asarray(col_ok), toep, MASKED)
    toep = jnp.concatenate([toep, jnp.full_like(toep[:, :, :1], MASKED)], axis=2)

    combos = {(n_dr, n_dr): 0}
    index = np.zeros((4, ur, NA_WIN_UNITS * ur // 2), np.int32)
    for case, g in enumerate((0, 1, g_units - 1)):
        ws = int(np.clip(g - 1, 0, g_units - NA_WIN_UNITS)) * ur
        for rho in range(ur):
            r = g * ur + rho
            rs = int(np.clip(r - NA_WIN_ROWS // 2, 0, rows - NA_WIN_ROWS))
            for kp in range(NA_WIN_UNITS * ur // 2):
                pair = []
                for kr in (ws + 2 * kp, ws + 2 * kp + 1):
                    ok = rs <= kr < rs + NA_WIN_ROWS
                    pair.append(kr - r + NA_WIN_ROWS - 1 if ok else n_dr)
                index[case, rho, kp] = combos.setdefault(tuple(pair), len(combos))
    left = np.array([c[0] for c in combos], np.int32)
    right = np.array([c[1] for c in combos], np.int32)
    tiles = jnp.concatenate([jnp.take(toep, left, axis=2), jnp.take(toep, right, axis=2)], axis=-1)
    depth = rpb.shape[0]
    tiles = tiles.reshape(depth, NA_HEADS // 2, 2, len(combos), GRID_W, 2 * GRID_W)
    return tiles, jnp.asarray(index)


def _na_call(nq, nk, nv, ng, tiles, index, n_lat, n_ctx, with_ctx):
    b, t, _ = nq.shape
    u = ROW_BLOCK
    g_units = n_lat // u
    n_units = (t if with_ctx else n_lat) // u

    def win(i):
        return lambda bi, g: (bi, jnp.clip(g - 1, 0, g_units - NA_WIN_UNITS) + i, 0)

    tok = pl.BlockSpec((1, u, NA_W), lambda bi, g: (bi, g, 0))
    ctx = pl.BlockSpec((1, n_ctx, NA_W), lambda bi, g: (bi, n_lat // n_ctx, 0))
    kv = [pl.BlockSpec((1, u, NA_W), win(i)) for i in range(NA_WIN_UNITS)]
    return pl.pallas_call(
        functools.partial(_na_kernel, g_units=g_units),
        grid=(b, n_units),
        in_specs=[pl.BlockSpec(memory_space=pltpu.SMEM), tok] + kv + kv
        + [ctx, ctx, pl.BlockSpec(tiles.shape, lambda bi, g: (0,) * tiles.ndim), tok],
        out_specs=tok,
        out_shape=jax.ShapeDtypeStruct((b, n_units * u, NA_W), BF16),
        compiler_params=_params(("parallel", "arbitrary")),
        name="nbr_attn",
    )(index, nq, nk, nk, nk, nv, nv, nv, nk, nv, tiles, ng)


def _gqa_kernel(bound_ref, q_ref, k_ref, v_ref, g_ref, o_ref, q2_sc, m_sc, l_sc, acc_sc, *,
                n_lat, tk_shift, tk_online):
    i = pl.program_id(1)
    tq = q_ref.shape[1]
    pairs = GQA_W // LANES
    lane = lax.broadcasted_iota(jnp.int32, (1, LANES), 1)
    mask_a = (lane % 64) < 32
    lo = lane < 64
    for j in range(pairs):
        q2_sc[j] = _pair_split(q_ref[0, :, j * LANES:(j + 1) * LANES], mask_a)
    t_all = k_ref.shape[1]
    is_latent = i * tq < n_lat
    bound = bound_ref[0]

    def over_keys(step, tk):
        @pl.when(is_latent)
        def _():
            def body(c, carry):
                off = pl.multiple_of(c * tk, tk)
                step(k_ref[0, pl.ds(off, tk), :], v_ref[0, pl.ds(off, tk), :])
                return carry
            lax.fori_loop(0, n_lat // tk, body, 0)
        step(k_ref[0, n_lat:t_all, :], v_ref[0, n_lat:t_all, :])

    def write_out(j, o2):
        sl = slice(j * LANES, (j + 1) * LANES)
        o = jnp.where(lo, o2[:tq], o2[tq:])
        o_ref[0, :, sl] = (o * g_ref[0, :, sl].astype(F32)).astype(BF16)

    @pl.when(bound <= GQA_SHIFT_LIMIT)
    def _():
        l_sc[...] = jnp.zeros_like(l_sc)
        acc_sc[...] = jnp.zeros_like(acc_sc)

        def step(k, v):
            width = k.shape[0]
            for j in range(pairs):
                p = jnp.exp2(_dot_t(q2_sc[j], k) - bound)
                part = p[:, :LANES]
                for cidx in range(1, width // LANES):
                    part = part + p[:, cidx * LANES:(cidx + 1) * LANES]
                l_sc[j] = l_sc[j] + part
                acc_sc[j] = acc_sc[j] + jnp.dot(p.astype(BF16), v, preferred_element_type=F32)

        over_keys(step, tk_shift)
        for j in range(pairs):
            write_out(j, acc_sc[j] / jnp.sum(l_sc[j], axis=-1, keepdims=True))

    @pl.when(bound > GQA_SHIFT_LIMIT)
    def _():
        m_sc[...] = jnp.full_like(m_sc, MASKED)
        l_sc[...] = jnp.zeros_like(l_sc)
        acc_sc[...] = jnp.zeros_like(acc_sc)

        def step(k, v):
            width = k.shape[0]
            for j in range(pairs):
                s = _dot_t(q2_sc[j], k)
                m_old = m_sc[j]
                m_new = jnp.maximum(m_old, jnp.max(s, axis=-1, keepdims=True))
                alpha = jnp.exp2(m_old - m_new)
                p = jnp.exp2(s - jnp.concatenate([m_new] * (width // LANES), axis=1))
                l_sc[j] = alpha * l_sc[j] + jnp.sum(p, axis=-1, keepdims=True)
                acc_sc[j] = alpha * acc_sc[j] + jnp.dot(p.astype(BF16), v, preferred_element_type=F32)
                m_sc[j] = m_new

        over_keys(step, tk_online)
        for j in range(pairs):
            write_out(j, acc_sc[j] / l_sc[j])


def _gqa_score_bound(q_norm_w, k_norm_w):
    bound = HEAD_D * jnp.max(jnp.abs(q_norm_w)) * jnp.max(jnp.abs(k_norm_w)) * (HEAD_D ** -0.5 * LOG2E)
    return (bound * 1.02).reshape(1).astype(F32)


def _gqa_call(bound, gq, gk, gv, gg, n_lat, with_ctx):
    b, t, _ = gq.shape
    tq = ROW_BLOCK
    t_out = t if with_ctx else n_lat
    pairs = GQA_W // LANES
    tok = pl.BlockSpec((1, tq, GQA_W), lambda bi, i: (bi, i, 0))
    kv = pl.BlockSpec((1, t, GQA_KV_W), lambda bi, i: (bi, 0, 0))
    return pl.pallas_call(
        functools.partial(_gqa_kernel, n_lat=n_lat, tk_shift=min(GQA_KEY_CHUNK, n_lat),
                          tk_online=min(GQA_ONLINE_KEY_CHUNK, n_lat)),
        grid=(b, t_out // tq),
        in_specs=[pl.BlockSpec(memory_space=pltpu.SMEM), tok, kv, kv, tok],
        out_specs=tok,
        out_shape=jax.ShapeDtypeStruct((b, t_out, GQA_W), BF16),
        scratch_shapes=[pltpu.VMEM((pairs, 2 * tq, LANES), BF16)] + [pltpu.VMEM((pairs, 2 * tq, LANES), F32)] * 3,
        compiler_params=_params(("parallel", "arbitrary")),
        name="gqa_attn",
    )(bound, gq, gk, gv, gg)


def _merge_kernel(x_ref, mod_ref, of_ref, ob_ref, rg_ref, gnw_ref, na_ref, gqa_ref, mg_ref,
                  wr_ref, wn_ref, wg_ref, wo_ref, fw_ref, o_ref, *, final):
    o = of_ref[0] + ob_ref[0]
    parts = []
    for h in range(RET_HEADS):
        oh = o[:, h * RET_D:(h + 1) * RET_D]
        mu = jnp.mean(oh, axis=-1, keepdims=True)
        dev = oh - mu
        var = jnp.mean(dev * dev, axis=-1, keepdims=True)
        parts.append(dev * lax.rsqrt(var + EPS))
    ret_x = (jnp.concatenate(parts, axis=-1) * gnw_ref[...] * rg_ref[0].astype(F32)).astype(BF16)
    m = mg_ref[0, :, :D_MODEL].astype(F32) * jnp.dot(ret_x, wr_ref[...], preferred_element_type=F32)
    m = m + mg_ref[0, :, D_MODEL:2 * D_MODEL].astype(F32) * jnp.dot(na_ref[0], wn_ref[...],
                                                                    preferred_element_type=F32)
    m = m + mg_ref[0, :, 2 * D_MODEL:].astype(F32) * jnp.dot(gqa_ref[0], wg_ref[...],
                                                             preferred_element_type=F32)
    out = jnp.dot(m.astype(BF16), wo_ref[...], preferred_element_type=F32)
    gate = mod_ref[0][:, 2 * D_MODEL:]
    xn = x_ref[0] + gate * out
    if final:
        ms = jnp.mean(xn * xn, axis=-1, keepdims=True)
        xn = xn * lax.rsqrt(ms + EPS) * fw_ref[...]
    o_ref[0] = xn


def _merge_call(xa, mod, ret_o, rg, gnw, na_o, gqa_o, mg, wr, wn, wg, wo, fw, n_lat, final):
    b, t, _ = xa.shape
    tm = ROW_BLOCK
    t_out = n_lat if final else t
    tok = lambda w: pl.BlockSpec((1, tm, w), lambda bi, i: (bi, i, 0))
    const2 = lambda shape: pl.BlockSpec(shape, lambda bi, i: (0, 0))
    return pl.pallas_call(
        functools.partial(_merge_kernel, final=final),
        grid=(b, t_out // tm),
        in_specs=[
            tok(D_MODEL),
            pl.BlockSpec((1, 1, 3 * D_MODEL), _mod_index(n_lat // tm, b)),
            tok(RET_W),
            tok(RET_W),
            tok(RET_W),
            const2((1, RET_W)),
            tok(NA_W),
            tok(GQA_W),
            tok(N_BRANCH * D_MODEL),
            const2((RET_W, D_MODEL)),
            const2((NA_W, D_MODEL)),
            const2((GQA_W, D_MODEL)),
            const2((D_MODEL, D_MODEL)),
            const2((1, D_MODEL)),
        ],
        out_specs=tok(D_MODEL),
        out_shape=jax.ShapeDtypeStruct((b, t_out, D_MODEL), F32),
        compiler_params=_params(("parallel", "parallel")),
        name="merge",
    )(xa, mod, ret_o[0], ret_o[1], rg, gnw, na_o, gqa_o, mg, wr, wn, wg, wo, fw)


def _forward(x, c, ctx, c_ctx, ada_w, ada_b, norm_w, w_in, ret_log_decay, ret_gn_w, na_rpb,
             q_norm_w, k_norm_w, w_ret_o, w_na_o, w_gqa_o, w_out, final_norm_w):
    b, n, _ = x.shape
    n_ctx = ctx.shape[1]
    depth = ada_w.shape[0]
    t = n + n_ctx
    rows = n // GRID_W
    assert n % ROW_BLOCK == 0 and n_ctx == ROW_BLOCK and n % n_ctx == 0
    assert n % min(GQA_KEY_CHUNK, n) == 0 and n % min(GQA_ONLINE_KEY_CHUNK, n) == 0
    assert rows >= NA_WIN_UNITS * NA_UNIT_ROWS and rows % NA_UNIT_ROWS == 0

    w_in_p = _permute_w_in(w_in).astype(BF16)
    qw = _k_layout(jnp.concatenate([q_norm_w, q_norm_w], axis=-1)).reshape(depth, 1, LANES)
    kw = _k_layout(jnp.concatenate([k_norm_w, k_norm_w], axis=-1)).reshape(depth, 1, LANES)
    w_ret_p = w_ret_o.astype(BF16)
    w_na_p = w_na_o.astype(BF16)
    w_gqa_p = _o_layout(w_gqa_o, axis=-2).astype(BF16)
    w_out_p = w_out.astype(BF16)
    na_tiles, na_index = _na_bias_tables(na_rpb, rows)

    cos, sin = _rope_tables(n, t)
    mod_rows = 8 * ((b + 1 + 7) // 8)
    c_rows = jnp.concatenate([c, c_ctx[None, :], jnp.zeros((mod_rows - b - 1, D_MODEL), F32)], axis=0)
    mod_all = _ada_call(c_rows, ada_w, ada_b)

    xa = jnp.concatenate([x, ctx], axis=1)
    for layer in range(depth):
        final = layer == depth - 1
        mod = mod_all[layer].reshape(mod_rows, 1, 3 * D_MODEL)
        (rq, rk, rv, rg, nq, nk, nv, ng, gq, gk, gv, gg, mg) = _inproj_call(
            xa, mod, norm_w[layer].reshape(1, D_MODEL), w_in_p[layer], cos, sin, qw[layer], kw[layer], n)
        ret_o = _ret_call(ret_log_decay[layer], rq, rk, rv, n)
        na_o = _na_call(nq, nk, nv, ng, na_tiles[layer], na_index, n, n_ctx, with_ctx=not final)
        gqa_o = _gqa_call(_gqa_score_bound(q_norm_w[layer], k_norm_w[layer]), gq, gk, gv, gg, n,
                          with_ctx=not final)
        xa = _merge_call(xa, mod, ret_o, rg, ret_gn_w[layer].reshape(1, RET_W), na_o, gqa_o, mg,
                         w_ret_p[layer], w_na_p[layer], w_gqa_p[layer], w_out_p[layer],
                         final_norm_w.reshape(1, D_MODEL), n, final)
    return xa


def kernel(x, c, ctx, c_ctx, ada_w, ada_b, norm_w, w_in, ret_log_decay, ret_gn_w, na_rpb, q_norm_w, k_norm_w,
           w_ret_o, w_na_o, w_gqa_o, w_out, final_norm_w):
    return _forward(x, c, ctx, c_ctx, ada_w, ada_b, norm_w, w_in, ret_log_decay, ret_gn_w, na_rpb,
                    q_norm_w, k_norm_w, w_ret_o, w_na_o, w_gqa_o, w_out, final_norm_w)
```

```python
import functools

import numpy as np
import jax
import jax.numpy as jnp
from jax import lax
from jax.experimental import pallas as pl
from jax.experimental.pallas import tpu as pltpu

F32 = jnp.float32
BF16 = jnp.bfloat16

D_MODEL = 1024
GRID_W = 64
EPS = 1e-6
ROPE_THETA = 10000.0
RET_HEADS = 4
RET_D = 128
RET_W = RET_HEADS * RET_D
RET_STEP = 256
NA_HEADS = 8
HEAD_D = 64
NA_W = NA_HEADS * HEAD_D
NA_WIN_ROWS = 8
NA_WIN_COLS = 16
GQA_HEADS = 8
GQA_KV_HEADS = 2
GQA_W = GQA_HEADS * HEAD_D
GQA_KV_W = GQA_KV_HEADS * HEAD_D
N_BRANCH = 3
LANES = 128
MXU_COLS = 256
ROW_BLOCK = 256
NA_UNIT_ROWS = ROW_BLOCK // GRID_W
NA_WIN_UNITS = 3
GQA_KEY_CHUNK = 4096
GQA_ONLINE_KEY_CHUNK = 2048
LOG2E = 1.4426950408889634
GQA_SHIFT_LIMIT = 40.0
MASKED = -1e30
VMEM_LIMIT = 56 * 1024 * 1024

_SPLITS = (RET_W, RET_W, RET_W, RET_W, NA_W, NA_W, NA_W, NA_W, GQA_W, GQA_KV_W, GQA_KV_W, GQA_W,
           N_BRANCH * D_MODEL)
_OFFS = np.concatenate([[0], np.cumsum(_SPLITS)])
IN_COLS = int(_OFFS[-1])
(O_RQ, O_RK, O_RV, O_RG, O_NQ, O_NK, O_NV, O_NG, O_GQ, O_GK, O_GV, O_GG, O_MG) = (int(o) for o in _OFFS[:-1])


def _params(sem, vmem=VMEM_LIMIT):
    return pltpu.CompilerParams(dimension_semantics=sem, vmem_limit_bytes=vmem)


def _silu(x):
    return x / (1.0 + jnp.exp(-x))


def _sigmoid(x):
    return 1.0 / (1.0 + jnp.exp(-x))


def _lane_dims():
    l = np.arange(LANES)
    half = l // 64
    slot = (l % 64) // 32
    r = l % 32
    d = np.where(r < 16, r, 32 + r - 16) + 16 * half
    return slot, d


def _transpose_cols(w, shape, order, axis=-1):
    axis = axis % w.ndim
    k = len(shape)
    full = w.shape[:axis] + tuple(shape) + w.shape[axis + 1:]
    perm = tuple(range(axis)) + tuple(axis + o for o in order) + tuple(range(axis + k, len(full)))
    return jnp.transpose(w.reshape(full), perm).reshape(w.shape)


def _q_layout(w):
    return _transpose_cols(w, (2, GQA_HEADS // 2, 2, 2, 16), (1, 3, 0, 2, 4))


def _k_layout(w):
    return _transpose_cols(w, (2, 2, 2, 16), (2, 0, 1, 3))


def _o_layout(w, axis=-1):
    return _transpose_cols(w, (2, GQA_HEADS // 2, HEAD_D), (1, 0, 2), axis=axis)


def _permute_w_in(w_in):
    parts = [w_in[..., :O_GQ], _q_layout(w_in[..., O_GQ:O_GK]), _k_layout(w_in[..., O_GK:O_GV]),
             w_in[..., O_GV:O_GG], _o_layout(w_in[..., O_GG:O_MG])]
    return jnp.concatenate(parts, axis=-1)


def _rope_tables(n, t_total):
    _, d = _lane_dims()
    quarter = HEAD_D // 4
    freqs = ROPE_THETA ** (-jnp.arange(quarter, dtype=F32) / quarter)
    t = jnp.arange(n)
    pos_r = (t // GRID_W).astype(F32)
    pos_c = (t % GRID_W).astype(F32)
    dd = d % 32
    fi = dd % quarter
    use_col = (d // 32) == 1
    first = dd < quarter
    ang = jnp.where(use_col[None, :], pos_c[:, None], pos_r[:, None]) * freqs[fi][None, :]
    cos = jnp.cos(ang)
    sin = jnp.where(first[None, :], -jnp.sin(ang), jnp.sin(ang))
    pad = t_total - n
    cos = jnp.concatenate([cos, jnp.ones((pad, LANES), F32)], axis=0)
    sin = jnp.concatenate([sin, jnp.zeros((pad, LANES), F32)], axis=0)
    return cos, sin


def _ada_kernel(c_ref, w_ref, b_ref, o_ref):
    cs = _silu(c_ref[...])
    o_ref[0] = jnp.dot(cs, w_ref[0], preferred_element_type=F32) + b_ref[0]


def _ada_call(c_rows, ada_w, ada_b):
    depth = ada_w.shape[0]
    rows = c_rows.shape[0]
    nb = 3
    return pl.pallas_call(
        _ada_kernel,
        grid=(depth, nb),
        in_specs=[
            pl.BlockSpec((rows, D_MODEL), lambda l, j: (0, 0)),
            pl.BlockSpec((1, D_MODEL, D_MODEL), lambda l, j: (l, 0, j)),
            pl.BlockSpec((1, 1, D_MODEL), lambda l, j: (l, 0, j)),
        ],
        out_specs=pl.BlockSpec((1, rows, D_MODEL), lambda l, j: (l, 0, j)),
        out_shape=jax.ShapeDtypeStruct((depth, rows, nb * D_MODEL), F32),
        compiler_params=_params(("parallel", "parallel")),
        name="ada_mod",
    )(c_rows, ada_w, ada_b.reshape(depth, 1, nb * D_MODEL))


def _head_mean_sq(t, mask_a):
    sq = t * t
    s_all = jnp.sum(sq, axis=-1, keepdims=True)
    s_a = jnp.sum(jnp.where(mask_a, sq, 0.0), axis=-1, keepdims=True)
    return jnp.where(mask_a, s_a, s_all - s_a) * (1.0 / HEAD_D)


def _norm_rope(t, w_lane, cos, sin, mask_a):
    tn = t * lax.rsqrt(_head_mean_sq(t, mask_a) + EPS) * w_lane
    return tn * cos + pltpu.roll(tn, LANES // 2, 1) * sin


def _inproj_kernel(x_ref, mod_ref, nw_ref, w_ref, cos_ref, sin_ref, qw_ref, kw_ref,
                   rq_ref, rk_ref, rv_ref, rg_ref, nq_ref, nk_ref, nv_ref, ng_ref,
                   gq_ref, gk_ref, gv_ref, gg_ref, h_ref):
    x = x_ref[0]
    ms = jnp.mean(x * x, axis=-1, keepdims=True)
    y = x * lax.rsqrt(ms + EPS) * nw_ref[...]
    mod = mod_ref[0]
    shift = mod[:, :D_MODEL]
    scale = mod[:, D_MODEL:2 * D_MODEL]
    h = (y * (1.0 + scale) + shift).astype(BF16)

    def proj(off, width):
        return jnp.dot(h, w_ref[:, off:off + width], preferred_element_type=F32)

    qk_scale = HEAD_D ** -0.5
    rq_ref[0] = proj(O_RQ, RET_W).astype(BF16)
    rk_ref[0] = proj(O_RK, RET_W).astype(BF16)
    rv_ref[0] = proj(O_RV, RET_W).astype(BF16)
    rg_ref[0] = _silu(proj(O_RG, RET_W)).astype(BF16)
    nq_ref[0] = (proj(O_NQ, NA_W) * (qk_scale * LOG2E)).astype(BF16)
    nk_ref[0] = proj(O_NK, NA_W).astype(BF16)
    nv_ref[0] = proj(O_NV, NA_W).astype(BF16)
    ng_ref[0] = _silu(proj(O_NG, NA_W)).astype(BF16)

    lane = lax.broadcasted_iota(jnp.int32, (1, LANES), 1)
    mask_a = (lane % 64) < 32
    cos = cos_ref[...]
    sin = sin_ref[...]
    qw = qw_ref[...]
    for j2 in range(GQA_W // MXU_COLS):
        t2 = proj(O_GQ + j2 * MXU_COLS, MXU_COLS)
        for j in range(2 * j2, 2 * j2 + 2):
            t = t2[:, (j % 2) * LANES:(j % 2 + 1) * LANES]
            gq_ref[0, :, j * LANES:(j + 1) * LANES] = (_norm_rope(t, qw, cos, sin, mask_a)
                                                       * (qk_scale * LOG2E)).astype(BF16)
    kv = proj(O_GK, 2 * GQA_KV_W)
    gk_ref[0] = _norm_rope(kv[:, :GQA_KV_W], kw_ref[...], cos, sin, mask_a).astype(BF16)
    gv_ref[0] = kv[:, GQA_KV_W:].astype(BF16)
    gg_ref[0] = _silu(proj(O_GG, GQA_W)).astype(BF16)
    h_ref[0] = h


def _mod_index(n_lat_blocks, batch):
    return lambda b, i: (jnp.where(i < n_lat_blocks, b, batch), 0, 0)


def _inproj_call(xa, mod, norm_w, w_in, cos, sin, qw, kw, n_lat):
    b, t, _ = xa.shape
    tm = ROW_BLOCK
    widths = (RET_W, RET_W, RET_W, RET_W, NA_W, NA_W, NA_W, NA_W, GQA_W, GQA_KV_W, GQA_KV_W, GQA_W, D_MODEL)
    tok = lambda w: pl.BlockSpec((1, tm, w), lambda bi, i: (bi, i, 0))
    const2 = lambda shape: pl.BlockSpec(shape, lambda bi, i: (0, 0))
    return pl.pallas_call(
        _inproj_kernel,
        grid=(b, t // tm),
        in_specs=[
            tok(D_MODEL),
            pl.BlockSpec((1, 1, 3 * D_MODEL), _mod_index(n_lat // tm, b)),
            const2((1, D_MODEL)),
            pl.BlockSpec((D_MODEL, O_MG), lambda bi, i: (0, 0), pipeline_mode=pl.Buffered(1)),
            pl.BlockSpec((tm, LANES), lambda bi, i: (i, 0)),
            pl.BlockSpec((tm, LANES), lambda bi, i: (i, 0)),
            const2((1, LANES)),
            const2((1, LANES)),
        ],
        out_specs=[tok(w) for w in widths],
        out_shape=[jax.ShapeDtypeStruct((b, t, w), BF16) for w in widths],
        compiler_params=_params(("parallel", "parallel")),
        name="in_proj",
    )(xa, mod, norm_w, w_in, cos, sin, qw, kw)


def _ret_kernel(rld_ref, qf_ref, kf_ref, vf_ref, qb_ref, kb_ref, vb_ref, of_ref, ob_ref,
                state, dec, xi, zeta, gch):
    s = pl.program_id(1)
    c = RET_STEP
    ksc = RET_D ** -0.5

    @pl.when(s == 0)
    def _():
        state[...] = jnp.zeros_like(state)
        ii = lax.broadcasted_iota(jnp.int32, (c, c), 0)
        jj = lax.broadcasted_iota(jnp.int32, (c, c), 1)
        ri = lax.broadcasted_iota(jnp.int32, (c, RET_D), 0)
        for d in range(2):
            delta = ((ii - jj) if d == 0 else (jj - ii)).astype(F32)
            ip = (ri if d == 0 else c - 1 - ri).astype(F32)
            for h in range(RET_HEADS):
                log_g = -jnp.exp(jnp.full((1, 1), rld_ref[d, h], F32))
                dec[d, h] = jnp.where(delta >= 0, jnp.exp(jnp.maximum(delta, 0.0) * log_g), 0.0) * ksc
                xi[d, h] = jnp.exp((ip + 1.0) * log_g)
                zeta[d, h] = jnp.exp((c - 1.0 - ip) * log_g) * ksc
                gch[d, h] = jnp.exp(jnp.full((RET_D, RET_D), float(c), F32) * log_g)

    for d, (q_ref, k_ref, v_ref, o_ref) in enumerate(((qf_ref, kf_ref, vf_ref, of_ref),
                                                      (qb_ref, kb_ref, vb_ref, ob_ref))):
        for h in range(RET_HEADS):
            sl = slice(h * RET_D, (h + 1) * RET_D)
            q = q_ref[0, :, sl]
            k = k_ref[0, :, sl]
            v = v_ref[0, :, sl]
            scores = lax.dot_general(q, k, (((1,), (1,)), ((), ())), preferred_element_type=F32)
            intra = jnp.dot((scores * dec[d, h]).astype(BF16), v, preferred_element_type=F32)
            st = state[d, h]
            inter = jnp.dot(q, st.astype(BF16), preferred_element_type=F32) * xi[d, h]
            o_ref[0, :, sl] = intra + inter
            vz = (v.astype(F32) * zeta[d, h]).astype(BF16)
            upd = lax.dot_general(k, vz, (((0,), (0,)), ((), ())), preferred_element_type=F32)
            state[d, h] = gch[d, h] * st + upd


def _ret_call(rld, rq, rk, rv, n_lat):
    b, t, _ = rq.shape
    c = RET_STEP
    nc = t // c
    ncn = n_lat // c
    fwd = pl.BlockSpec((1, c, RET_W), lambda bi, s: (bi, (s + ncn) % nc, 0))
    bwd = pl.BlockSpec((1, c, RET_W), lambda bi, s: (bi, nc - 1 - s, 0))
    per_head = lambda rows, cols: pltpu.VMEM((2, RET_HEADS, rows, cols), F32)
    out = jax.ShapeDtypeStruct((b, t, RET_W), F32)
    return pl.pallas_call(
        _ret_kernel,
        grid=(b, nc),
        in_specs=[pl.BlockSpec(memory_space=pltpu.SMEM), fwd, fwd, fwd, bwd, bwd, bwd],
        out_specs=[fwd, bwd],
        out_shape=[out, out],
        scratch_shapes=[per_head(RET_D, RET_D), per_head(c, c), per_head(c, RET_D), per_head(c, RET_D),
                        per_head(RET_D, RET_D)],
        compiler_params=_params(("parallel", "arbitrary")),
        name="retention",
    )(rld, rq, rk, rv, rq, rk, rv)


def _pair_split(q, mask_a):
    zero = jnp.zeros_like(q)
    return jnp.concatenate([jnp.where(mask_a, q, zero), jnp.where(mask_a, zero, q)], axis=0)


def _dot_t(a, b):
    return lax.dot_general(a, b, (((1,), (1,)), ((), ())), preferred_element_type=F32)


def _na_kernel(idx_ref, q_ref, k0_ref, k1_ref, k2_ref, v0_ref, v1_ref, v2_ref, kc_ref, vc_ref, bias_ref,
               g_ref, o_ref, *, g_units):
    m = q_ref.shape[1]
    u = ROW_BLOCK
    n_ctx = kc_ref.shape[1]
    g = pl.program_id(1)
    case = jnp.where(g == 0, 0, jnp.where(g < g_units - 1, 1, jnp.where(g == g_units - 1, 2, 3)))
    tile_of = [[idx_ref[case, rho, kp] for kp in range(NA_WIN_UNITS * NA_UNIT_ROWS // 2)]
               for rho in range(NA_UNIT_ROWS)]
    lane = lax.broadcasted_iota(jnp.int32, (1, LANES), 1)
    lo = lane < 64
    pairs = NA_W // LANES
    slices = [slice(j * LANES, (j + 1) * LANES) for j in range(pairs)]
    raw = [_dot_t(_pair_split(q_ref[0, :, sl], lo),
                  jnp.concatenate([r[0, :, sl] for r in (kc_ref, k0_ref, k1_ref, k2_ref)], axis=0))
           for sl in slices]
    for j, sl in enumerate(slices):

        def bias(i):
            return jnp.concatenate(
                [jnp.concatenate([bias_ref[j, x, tile_of[rho][2 * i]], bias_ref[j, x, tile_of[rho][2 * i + 1]]], axis=1)
                 for x in range(2) for rho in range(NA_UNIT_ROWS)], axis=0)

        v_all = jnp.concatenate([r[0, :, sl] for r in (vc_ref, v0_ref, v1_ref, v2_ref)], axis=0)
        s = raw[j]
        s = jnp.concatenate([s[:, :n_ctx]] + [s[:, n_ctx + i * u:n_ctx + (i + 1) * u] + bias(i)
                                              for i in range(NA_WIN_UNITS)], axis=1)
        p = jnp.exp2(s - jnp.max(s, axis=-1, keepdims=True))
        den = jnp.sum(p, axis=-1, keepdims=True)
        acc = jnp.dot(p.astype(BF16), v_all, preferred_element_type=F32)
        o2 = acc / den
        o = jnp.where(lo, o2[:m], o2[m:])
        o_ref[0, :, sl] = (o * g_ref[0, :, sl].astype(F32)).astype(BF16)


def _na_bias_tables(rpb, rows):
    ur = NA_UNIT_ROWS
    g_units = rows // ur
    n_dr = 2 * NA_WIN_ROWS - 1
    n_dc = 2 * NA_WIN_COLS - 1
    qc = np.arange(GRID_W)
    cstart = np.clip(qc - NA_WIN_COLS // 2, 0, GRID_W - NA_WIN_COLS)
    col_ok = (qc[None, :] >= cstart[:, None]) & (qc[None, :] < cstart[:, None] + NA_WIN_COLS)
    dc = qc[None, :] - qc[:, None] + NA_WIN_COLS - 1
    hot_c = ((dc[None] == np.arange(n_dc)[:, None, None]) & col_ok[None]).astype(np.float32)
    hi = lax.Precision.HIGHEST
    toep = jnp.einsum('lhrd,dqj->lhrqj', rpb.astype(F32) * LOG2E, jnp.asarray(hot_c), precision=hi)
    toep = jnp.where(jnp.asarray(col_ok), toep, MASKED)
    toep = jnp.concatenate([toep, jnp.full_like(toep[:, :, :1], MASKED)], axis=2)

    combos = {(n_dr, n_dr): 0}
    index = np.zeros((4, ur, NA_WIN_UNITS * ur // 2), np.int32)
    for case, g in enumerate((0, 1, g_units - 1)):
        ws = int(np.clip(g - 1, 0, g_units - NA_WIN_UNITS)) * ur
        for rho in range(ur):
            r = g * ur + rho
            rs = int(np.clip(r - NA_WIN_ROWS // 2, 0, rows - NA_WIN_ROWS))
            for kp in range(NA_WIN_UNITS * ur // 2):
                pair = []
                for kr in (ws + 2 * kp, ws + 2 * kp + 1):
                    ok = rs <= kr < rs + NA_WIN_ROWS
                    pair.append(kr - r + NA_WIN_ROWS - 1 if ok else n_dr)
                index[case, rho, kp] = combos.setdefault(tuple(pair), len(combos))
    left = np.array([c[0] for c in combos], np.int32)
    right = np.array([c[1] for c in combos], np.int32)
    tiles = jnp.concatenate([jnp.take(toep, left, axis=2), jnp.take(toep, right, axis=2)], axis=-1)
    depth = rpb.shape[0]
    tiles = tiles.reshape(depth, NA_HEADS // 2, 2, len(combos), GRID_W, 2 * GRID_W)
    return tiles, jnp.asarray(index)


def _na_call(nq, nk, nv, ng, tiles, index, n_lat, n_ctx, with_ctx):
    b, t, _ = nq.shape
    u = ROW_BLOCK
    g_units = n_lat // u
    n_units = (t if with_ctx else n_lat) // u

    def win(i):
        return lambda bi, g: (bi, jnp.clip(g - 1, 0, g_units - NA_WIN_UNITS) + i, 0)

    tok = pl.BlockSpec((1, u, NA_W), lambda bi, g: (bi, g, 0))
    ctx = pl.BlockSpec((1, n_ctx, NA_W), lambda bi, g: (bi, n_lat // n_ctx, 0))
    kv = [pl.BlockSpec((1, u, NA_W), win(i)) for i in range(NA_WIN_UNITS)]
    return pl.pallas_call(
        functools.partial(_na_kernel, g_units=g_units),
        grid=(b, n_units),
        in_specs=[pl.BlockSpec(memory_space=pltpu.SMEM), tok] + kv + kv
        + [ctx, ctx, pl.BlockSpec(tiles.shape, lambda bi, g: (0,) * tiles.ndim), tok],
        out_specs=tok,
        out_shape=jax.ShapeDtypeStruct((b, n_units * u, NA_W), BF16),
        compiler_params=_params(("parallel", "arbitrary")),
        name="nbr_attn",
    )(index, nq, nk, nk, nk, nv, nv, nv, nk, nv, tiles, ng)


def _gqa_kernel(bound_ref, q_ref, k_ref, v_ref, g_ref, o_ref, q2_sc, m_sc, l_sc, acc_sc, *,
                n_lat, tk_shift, tk_online):
    i = pl.program_id(1)
    tq = q_ref.shape[1]
    pairs = GQA_W // LANES
    lane = lax.broadcasted_iota(jnp.int32, (1, LANES), 1)
    mask_a = (lane % 64) < 32
    lo = lane < 64
    for j in range(pairs):
        q2_sc[j] = _pair_split(q_ref[0, :, j * LANES:(j + 1) * LANES], mask_a)
    t_all = k_ref.shape[1]
    is_latent = i * tq < n_lat
    bound = bound_ref[0]

    def over_keys(step, tk):
        @pl.when(is_latent)
        def _():
            def body(c, carry):
                off = pl.multiple_of(c * tk, tk)
                step(k_ref[0, pl.ds(off, tk), :], v_ref[0, pl.ds(off, tk), :])
                return carry
            lax.fori_loop(0, n_lat // tk, body, 0)
        step(k_ref[0, n_lat:t_all, :], v_ref[0, n_lat:t_all, :])

    def write_out(j, o2):
        sl = slice(j * LANES, (j + 1) * LANES)
        o = jnp.where(lo, o2[:tq], o2[tq:])
        o_ref[0, :, sl] = (o * g_ref[0, :, sl].astype(F32)).astype(BF16)

    @pl.when(bound <= GQA_SHIFT_LIMIT)
    def _():
        l_sc[...] = jnp.zeros_like(l_sc)
        acc_sc[...] = jnp.zeros_like(acc_sc)

        def step(k, v):
            width = k.shape[0]
            for j in range(pairs):
                p = jnp.exp2(_dot_t(q2_sc[j], k) - bound)
                part = p[:, :LANES]
                for cidx in range(1, width // LANES):
                    part = part + p[:, cidx * LANES:(cidx + 1) * LANES]
                l_sc[j] = l_sc[j] + part
                acc_sc[j] = acc_sc[j] + jnp.dot(p.astype(BF16), v, preferred_element_type=F32)

        over_keys(step, tk_shift)
        for j in range(pairs):
            write_out(j, acc_sc[j] / jnp.sum(l_sc[j], axis=-1, keepdims=True))

    @pl.when(bound > GQA_SHIFT_LIMIT)
    def _():
        m_sc[...] = jnp.full_like(m_sc, MASKED)
        l_sc[...] = jnp.zeros_like(l_sc)
        acc_sc[...] = jnp.zeros_like(acc_sc)

        def step(k, v):
            width = k.shape[0]
            for j in range(pairs):
                s = _dot_t(q2_sc[j], k)
                m_old = m_sc[j]
                m_new = jnp.maximum(m_old, jnp.max(s, axis=-1, keepdims=True))
                alpha = jnp.exp2(m_old - m_new)
                p = jnp.exp2(s - jnp.concatenate([m_new] * (width // LANES), axis=1))
                l_sc[j] = alpha * l_sc[j] + jnp.sum(p, axis=-1, keepdims=True)
                acc_sc[j] = alpha * acc_sc[j] + jnp.dot(p.astype(BF16), v, preferred_element_type=F32)
                m_sc[j] = m_new

        over_keys(step, tk_online)
        for j in range(pairs):
            write_out(j, acc_sc[j] / l_sc[j])


def _gqa_score_bound(q_norm_w, k_norm_w):
    bound = HEAD_D * jnp.max(jnp.abs(q_norm_w)) * jnp.max(jnp.abs(k_norm_w)) * (HEAD_D ** -0.5 * LOG2E)
    return (bound * 1.02).reshape(1).astype(F32)


def _gqa_call(bound, gq, gk, gv, gg, n_lat, with_ctx):
    b, t, _ = gq.shape
    tq = ROW_BLOCK
    t_out = t if with_ctx else n_lat
    pairs = GQA_W // LANES
    tok = pl.BlockSpec((1, tq, GQA_W), lambda bi, i: (bi, i, 0))
    kv = pl.BlockSpec((1, t, GQA_KV_W), lambda bi, i: (bi, 0, 0))
    return pl.pallas_call(
        functools.partial(_gqa_kernel, n_lat=n_lat, tk_shift=min(GQA_KEY_CHUNK, n_lat),
                          tk_online=min(GQA_ONLINE_KEY_CHUNK, n_lat)),
        grid=(b, t_out // tq),
        in_specs=[pl.BlockSpec(memory_space=pltpu.SMEM), tok, kv, kv, tok],
        out_specs=tok,
        out_shape=jax.ShapeDtypeStruct((b, t_out, GQA_W), BF16),
        scratch_shapes=[pltpu.VMEM((pairs, 2 * tq, LANES), BF16)] + [pltpu.VMEM((pairs, 2 * tq, LANES), F32)] * 3,
        compiler_params=_params(("parallel", "arbitrary")),
        name="gqa_attn",
    )(bound, gq, gk, gv, gg)


def _merge_kernel(x_ref, mod_ref, of_ref, ob_ref, rg_ref, gnw_ref, na_ref, gqa_ref, h_ref,
                  wm_ref, wr_ref, wn_ref, wg_ref, wo_ref, fw_ref, o_ref, *, final):
    def branch_gate(j):
        return _sigmoid(jnp.dot(h_ref[0], wm_ref[:, j * D_MODEL:(j + 1) * D_MODEL], preferred_element_type=F32))

    o = of_ref[0] + ob_ref[0]
    parts = []
    for h in range(RET_HEADS):
        oh = o[:, h * RET_D:(h + 1) * RET_D]
        mu = jnp.mean(oh, axis=-1, keepdims=True)
        dev = oh - mu
        var = jnp.mean(dev * dev, axis=-1, keepdims=True)
        parts.append(dev * lax.rsqrt(var + EPS))
    ret_x = (jnp.concatenate(parts, axis=-1) * gnw_ref[...] * rg_ref[0].astype(F32)).astype(BF16)
    m = branch_gate(0) * jnp.dot(ret_x, wr_ref[...], preferred_element_type=F32)
    m = m + branch_gate(1) * jnp.dot(na_ref[0], wn_ref[...], preferred_element_type=F32)
    m = m + branch_gate(2) * jnp.dot(gqa_ref[0], wg_ref[...], preferred_element_type=F32)
    out = jnp.dot(m.astype(BF16), wo_ref[...], preferred_element_type=F32)
    gate = mod_ref[0][:, 2 * D_MODEL:]
    xn = x_ref[0] + gate * out
    if final:
        ms = jnp.mean(xn * xn, axis=-1, keepdims=True)
        xn = xn * lax.rsqrt(ms + EPS) * fw_ref[...]
    o_ref[0] = xn


def _merge_call(xa, mod, ret_o, rg, gnw, na_o, gqa_o, h, wm, wr, wn, wg, wo, fw, n_lat, final):
    b, t, _ = xa.shape
    tm = ROW_BLOCK
    t_out = n_lat if final else t
    tok = lambda w: pl.BlockSpec((1, tm, w), lambda bi, i: (bi, i, 0))
    const2 = lambda shape: pl.BlockSpec(shape, lambda bi, i: (0, 0))
    return pl.pallas_call(
        functools.partial(_merge_kernel, final=final),
        grid=(b, t_out // tm),
        in_specs=[
            tok(D_MODEL),
            pl.BlockSpec((1, 1, 3 * D_MODEL), _mod_index(n_lat // tm, b)),
            tok(RET_W),
            tok(RET_W),
            tok(RET_W),
            const2((1, RET_W)),
            tok(NA_W),
            tok(GQA_W),
            tok(D_MODEL),
            const2((D_MODEL, N_BRANCH * D_MODEL)),
            const2((RET_W, D_MODEL)),
            const2((NA_W, D_MODEL)),
            const2((GQA_W, D_MODEL)),
            const2((D_MODEL, D_MODEL)),
            const2((1, D_MODEL)),
        ],
        out_specs=tok(D_MODEL),
        out_shape=jax.ShapeDtypeStruct((b, t_out, D_MODEL), F32),
        compiler_params=_params(("parallel", "parallel")),
        name="merge",
    )(xa, mod, ret_o[0], ret_o[1], rg, gnw, na_o, gqa_o, h, wm, wr, wn, wg, wo, fw)


def _forward(x, c, ctx, c_ctx, ada_w, ada_b, norm_w, w_in, ret_log_decay, ret_gn_w, na_rpb,
             q_norm_w, k_norm_w, w_ret_o, w_na_o, w_gqa_o, w_out, final_norm_w):
    b, n, _ = x.shape
    n_ctx = ctx.shape[1]
    depth = ada_w.shape[0]
    t = n + n_ctx
    rows = n // GRID_W
    assert n % ROW_BLOCK == 0 and n_ctx == ROW_BLOCK and n % n_ctx == 0
    assert n % min(GQA_KEY_CHUNK, n) == 0 and n % min(GQA_ONLINE_KEY_CHUNK, n) == 0
    assert rows >= NA_WIN_UNITS * NA_UNIT_ROWS and rows % NA_UNIT_ROWS == 0

    w_in_p = _permute_w_in(w_in).astype(BF16)
    w_mg_p = w_in[..., O_MG:].astype(BF16)
    qw =_k_layout(jnp.concatenate([q_norm_w, q_norm_w], axis=-1)).reshape(depth, 1, LANES)
    kw = _k_layout(jnp.concatenate([k_norm_w, k_norm_w], axis=-1)).reshape(depth, 1, LANES)
    w_ret_p = w_ret_o.astype(BF16)
    w_na_p = w_na_o.astype(BF16)
    w_gqa_p = _o_layout(w_gqa_o, axis=-2).astype(BF16)
    w_out_p = w_out.astype(BF16)
    na_tiles, na_index = _na_bias_tables(na_rpb, rows)

    cos, sin = _rope_tables(n, t)
    mod_rows = 8 * ((b + 1 + 7) // 8)
    c_rows = jnp.concatenate([c, c_ctx[None, :], jnp.zeros((mod_rows - b - 1, D_MODEL), F32)], axis=0)
    mod_all = _ada_call(c_rows, ada_w, ada_b)

    xa = jnp.concatenate([x, ctx], axis=1)
    for layer in range(depth):
        final = layer == depth - 1
        mod = mod_all[layer].reshape(mod_rows, 1, 3 * D_MODEL)
        (rq, rk, rv, rg, nq, nk, nv, ng, gq, gk, gv, gg, h) = _inproj_call(
            xa, mod, norm_w[layer].reshape(1, D_MODEL), w_in_p[layer], cos, sin, qw[layer], kw[layer], n)
        ret_o = _ret_call(ret_log_decay[layer], rq, rk, rv, n)
        na_o = _na_call(nq, nk, nv, ng, na_tiles[layer], na_index, n, n_ctx, with_ctx=not final)
        gqa_o = _gqa_call(_gqa_score_bound(q_norm_w[layer], k_norm_w[layer]), gq, gk, gv, gg, n,
                          with_ctx=not final)
        xa = _merge_call(xa, mod, ret_o, rg, ret_gn_w[layer].reshape(1, RET_W), na_o, gqa_o, h,
                         w_mg_p[layer], w_ret_p[layer], w_na_p[layer], w_gqa_p[layer], w_out_p[layer],
                         final_norm_w.reshape(1, D_MODEL), n, final)
    return xa


def kernel(x, c, ctx, c_ctx, ada_w, ada_b, norm_w, w_in, ret_log_decay, ret_gn_w, na_rpb, q_norm_w, k_norm_w,
           w_ret_o, w_na_o, w_gqa_o, w_out, final_norm_w):
    return _forward(x, c, ctx, c_ctx, ada_w, ada_b, norm_w, w_in, ret_log_decay, ret_gn_w, na_rpb,
                    q_norm_w, k_norm_w, w_ret_o, w_na_o, w_gqa_o, w_out, final_norm_w)
```

```python
import functools

import numpy as np
import jax
import jax.numpy as jnp
from jax import lax
from jax.experimental import pallas as pl
from jax.experimental.pallas import tpu as pltpu

F32 = jnp.float32
BF16 = jnp.bfloat16

D_MODEL = 1024
GRID_W = 64
EPS = 1e-6
ROPE_THETA = 10000.0
RET_HEADS = 4
RET_D = 128
RET_W = RET_HEADS * RET_D
RET_STEP = 256
NA_HEADS = 8
HEAD_D = 64
NA_W = NA_HEADS * HEAD_D
NA_WIN_ROWS = 8
NA_WIN_COLS = 16
GQA_HEADS = 8
GQA_KV_HEADS = 2
GQA_W = GQA_HEADS * HEAD_D
GQA_KV_W = GQA_KV_HEADS * HEAD_D
N_BRANCH = 3
LANES = 128
MXU_COLS = 256
ROW_BLOCK = 256
NA_UNIT_ROWS = ROW_BLOCK // GRID_W
NA_WIN_UNITS = 3
GQA_KEY_CHUNK = 4096
GQA_ONLINE_KEY_CHUNK = 2048
LOG2E = 1.4426950408889634
GQA_SHIFT_LIMIT = 40.0
MASKED = -1e30
VMEM_LIMIT = 56 * 1024 * 1024

_SPLITS = (RET_W, RET_W, RET_W, RET_W, NA_W, NA_W, NA_W, NA_W, GQA_W, GQA_KV_W, GQA_KV_W, GQA_W,
           N_BRANCH * D_MODEL)
_OFFS = np.concatenate([[0], np.cumsum(_SPLITS)])
IN_COLS = int(_OFFS[-1])
(O_RQ, O_RK, O_RV, O_RG, O_NQ, O_NK, O_NV, O_NG, O_GQ, O_GK, O_GV, O_GG, O_MG) = (int(o) for o in _OFFS[:-1])


def _params(sem, vmem=VMEM_LIMIT):
    return pltpu.CompilerParams(dimension_semantics=sem, vmem_limit_bytes=vmem)


def _silu(x):
    return x / (1.0 + jnp.exp(-x))


def _sigmoid(x):
    return 1.0 / (1.0 + jnp.exp(-x))


def _lane_dims():
    l = np.arange(LANES)
    half = l // 64
    slot = (l % 64) // 32
    r = l % 32
    d = np.where(r < 16, r, 32 + r - 16) + 16 * half
    return slot, d


def _transpose_cols(w, shape, order, axis=-1):
    axis = axis % w.ndim
    k = len(shape)
    full = w.shape[:axis] + tuple(shape) + w.shape[axis + 1:]
    perm = tuple(range(axis)) + tuple(axis + o for o in order) + tuple(range(axis + k, len(full)))
    return jnp.transpose(w.reshape(full), perm).reshape(w.shape)


def _q_layout(w):
    return _transpose_cols(w, (2, GQA_HEADS // 2, 2, 2, 16), (1, 3, 0, 2, 4))


def _k_layout(w):
    return _transpose_cols(w, (2, 2, 2, 16), (2, 0, 1, 3))


def _o_layout(w, axis=-1):
    return _transpose_cols(w, (2, GQA_HEADS // 2, HEAD_D), (1, 0, 2), axis=axis)


def _permute_w_in(w_in):
    parts = [w_in[..., :O_GQ], _q_layout(w_in[..., O_GQ:O_GK]), _k_layout(w_in[..., O_GK:O_GV]),
             w_in[..., O_GV:O_GG], _o_layout(w_in[..., O_GG:O_MG])]
    return jnp.concatenate(parts, axis=-1)


def _rope_tables(n, t_total):
    _, d = _lane_dims()
    quarter = HEAD_D // 4
    freqs = ROPE_THETA ** (-jnp.arange(quarter, dtype=F32) / quarter)
    t = jnp.arange(n)
    pos_r = (t // GRID_W).astype(F32)
    pos_c = (t % GRID_W).astype(F32)
    dd = d % 32
    fi = dd % quarter
    use_col = (d // 32) == 1
    first = dd < quarter
    ang = jnp.where(use_col[None, :], pos_c[:, None], pos_r[:, None]) * freqs[fi][None, :]
    cos = jnp.cos(ang)
    sin = jnp.where(first[None, :], -jnp.sin(ang), jnp.sin(ang))
    pad = t_total - n
    cos = jnp.concatenate([cos, jnp.ones((pad, LANES), F32)], axis=0)
    sin = jnp.concatenate([sin, jnp.zeros((pad, LANES), F32)], axis=0)
    return cos, sin


def _ada_kernel(c_ref, w_ref, b_ref, o_ref):
    cs = _silu(c_ref[...])
    o_ref[0] = jnp.dot(cs, w_ref[0], preferred_element_type=F32) + b_ref[0]


def _ada_call(c_rows, ada_w, ada_b):
    depth = ada_w.shape[0]
    rows = c_rows.shape[0]
    nb = 3
    return pl.pallas_call(
        _ada_kernel,
        grid=(depth, nb),
        in_specs=[
            pl.BlockSpec((rows, D_MODEL), lambda l, j: (0, 0)),
            pl.BlockSpec((1, D_MODEL, D_MODEL), lambda l, j: (l, 0, j)),
            pl.BlockSpec((1, 1, D_MODEL), lambda l, j: (l, 0, j)),
        ],
        out_specs=pl.BlockSpec((1, rows, D_MODEL), lambda l, j: (l, 0, j)),
        out_shape=jax.ShapeDtypeStruct((depth, rows, nb * D_MODEL), F32),
        compiler_params=_params(("parallel", "parallel")),
        name="ada_mod",
    )(c_rows, ada_w, ada_b.reshape(depth, 1, nb * D_MODEL))


def _head_mean_sq(t, mask_a):
    sq = t * t
    s_all = jnp.sum(sq, axis=-1, keepdims=True)
    s_a = jnp.sum(jnp.where(mask_a, sq, 0.0), axis=-1, keepdims=True)
    return jnp.where(mask_a, s_a, s_all - s_a) * (1.0 / HEAD_D)


def _norm_rope(t, w_lane, cos, sin, mask_a):
    tn = t * lax.rsqrt(_head_mean_sq(t, mask_a) + EPS) * w_lane
    return tn * cos + pltpu.roll(tn, LANES // 2, 1) * sin


def _inproj_kernel(x_ref, mod_ref, nw_ref, w_ref, cos_ref, sin_ref, qw_ref, kw_ref,
                   rq_ref, rk_ref, rv_ref, rg_ref, nq_ref, nk_ref, nv_ref, ng_ref,
                   gq_ref, gk_ref, gv_ref, gg_ref, h_ref):
    x = x_ref[0]
    ms = jnp.mean(x * x, axis=-1, keepdims=True)
    y = x * lax.rsqrt(ms + EPS) * nw_ref[...]
    mod = mod_ref[0]
    shift = mod[:, :D_MODEL]
    scale = mod[:, D_MODEL:2 * D_MODEL]
    h = (y * (1.0 + scale) + shift).astype(BF16)

    def proj(off, width):
        return jnp.dot(h, w_ref[:, off:off + width], preferred_element_type=F32)

    qk_scale = HEAD_D ** -0.5
    rq_ref[0] = proj(O_RQ, RET_W).astype(BF16)
    rk_ref[0] = proj(O_RK, RET_W).astype(BF16)
    rv_ref[0] = proj(O_RV, RET_W).astype(BF16)
    rg_ref[0] = _silu(proj(O_RG, RET_W)).astype(BF16)
    nq_ref[0] = (proj(O_NQ, NA_W) * (qk_scale * LOG2E)).astype(BF16)
    nk_ref[0] = proj(O_NK, NA_W).astype(BF16)
    nv_ref[0] = proj(O_NV, NA_W).astype(BF16)
    ng_ref[0] = _silu(proj(O_NG, NA_W)).astype(BF16)

    lane = lax.broadcasted_iota(jnp.int32, (1, LANES), 1)
    mask_a = (lane % 64) < 32
    cos = cos_ref[...]
    sin = sin_ref[...]
    qw = qw_ref[...]
    for j2 in range(GQA_W // MXU_COLS):
        t2 = proj(O_GQ + j2 * MXU_COLS, MXU_COLS)
        for j in range(2 * j2, 2 * j2 + 2):
            t = t2[:, (j % 2) * LANES:(j % 2 + 1) * LANES]
            gq_ref[0, :, j * LANES:(j + 1) * LANES] = (_norm_rope(t, qw, cos, sin, mask_a)
                                                       * (qk_scale * LOG2E)).astype(BF16)
    kv = proj(O_GK, 2 * GQA_KV_W)
    gk_ref[0] = _norm_rope(kv[:, :GQA_KV_W], kw_ref[...], cos, sin, mask_a).astype(BF16)
    gv_ref[0] = kv[:, GQA_KV_W:].astype(BF16)
    gg_ref[0] = _silu(proj(O_GG, GQA_W)).astype(BF16)
    h_ref[0] = h


def _mod_index(n_lat_blocks, batch):
    return lambda b, i: (jnp.where(i < n_lat_blocks, b, batch), 0, 0)


def _inproj_call(xa, mod, norm_w, w_in, cos, sin, qw, kw, n_lat):
    b, t, _ = xa.shape
    tm = ROW_BLOCK
    widths = (RET_W, RET_W, RET_W, RET_W, NA_W, NA_W, NA_W, NA_W, GQA_W, GQA_KV_W, GQA_KV_W, GQA_W, D_MODEL)
    tok = lambda w: pl.BlockSpec((1, tm, w), lambda bi, i: (bi, i, 0))
    const2 = lambda shape: pl.BlockSpec(shape, lambda bi, i: (0, 0))
    return pl.pallas_call(
        _inproj_kernel,
        grid=(b, t // tm),
        in_specs=[
            tok(D_MODEL),
            pl.BlockSpec((1, 1, 3 * D_MODEL), _mod_index(n_lat // tm, b)),
            const2((1, D_MODEL)),
            pl.BlockSpec((D_MODEL, O_MG), lambda bi, i: (0, 0), pipeline_mode=pl.Buffered(1)),
            pl.BlockSpec((tm, LANES), lambda bi, i: (i, 0)),
            pl.BlockSpec((tm, LANES), lambda bi, i: (i, 0)),
            const2((1, LANES)),
            const2((1, LANES)),
        ],
        out_specs=[tok(w) for w in widths],
        out_shape=[jax.ShapeDtypeStruct((b, t, w), BF16) for w in widths],
        compiler_params=_params(("parallel", "parallel")),
        name="in_proj",
    )(xa, mod, norm_w, w_in, cos, sin, qw, kw)


def _ret_kernel(rld_ref, qf_ref, kf_ref, vf_ref, qb_ref, kb_ref, vb_ref, of_ref, ob_ref,
                state, dec, xi, zeta, gch):
    s = pl.program_id(1)
    c = RET_STEP
    ksc = RET_D ** -0.5

    @pl.when(s == 0)
    def _():
        state[...] = jnp.zeros_like(state)
        ii = lax.broadcasted_iota(jnp.int32, (c, c), 0)
        jj = lax.broadcasted_iota(jnp.int32, (c, c), 1)
        ri = lax.broadcasted_iota(jnp.int32, (c, RET_D), 0)
        for d in range(2):
            delta = ((ii - jj) if d == 0 else (jj - ii)).astype(F32)
            ip = (ri if d == 0 else c - 1 - ri).astype(F32)
            for h in range(RET_HEADS):
                log_g = -jnp.exp(jnp.full((1, 1), rld_ref[d, h], F32))
                dec[d, h] = jnp.where(delta >= 0, jnp.exp(jnp.maximum(delta, 0.0) * log_g), 0.0) * ksc
                xi[d, h] = jnp.exp((ip + 1.0) * log_g)
                zeta[d, h] = jnp.exp((c - 1.0 - ip) * log_g) * ksc
                gch[d, h] = jnp.exp(jnp.full((RET_D, RET_D), float(c), F32) * log_g)

    for d, (q_ref, k_ref, v_ref, o_ref) in enumerate(((qf_ref, kf_ref, vf_ref, of_ref),
                                                      (qb_ref, kb_ref, vb_ref, ob_ref))):
        for h in range(RET_HEADS):
            sl = slice(h * RET_D, (h + 1) * RET_D)
            q = q_ref[0, :, sl]
            k = k_ref[0, :, sl]
            v = v_ref[0, :, sl]
            scores = lax.dot_general(q, k, (((1,), (1,)), ((), ())), preferred_element_type=F32)
            intra = jnp.dot((scores * dec[d, h]).astype(BF16), v, preferred_element_type=F32)
            st = state[d, h]
            inter = jnp.dot(q, st.astype(BF16), preferred_element_type=F32) * xi[d, h]
            o_ref[0, :, sl] = intra + inter
            vz = (v.astype(F32) * zeta[d, h]).astype(BF16)
            upd = lax.dot_general(k, vz, (((0,), (0,)), ((), ())), preferred_element_type=F32)
            state[d, h] = gch[d, h] * st + upd


def _ret_call(rld, rq, rk, rv, n_lat):
    b, t, _ = rq.shape
    c = RET_STEP
    nc = t // c
    ncn = n_lat // c
    fwd = pl.BlockSpec((1, c, RET_W), lambda bi, s: (bi, (s + ncn) % nc, 0))
    bwd = pl.BlockSpec((1, c, RET_W), lambda bi, s: (bi, nc - 1 - s, 0))
    per_head = lambda rows, cols: pltpu.VMEM((2, RET_HEADS, rows, cols), F32)
    out = jax.ShapeDtypeStruct((b, t, RET_W), F32)
    return pl.pallas_call(
        _ret_kernel,
        grid=(b, nc),
        in_specs=[pl.BlockSpec(memory_space=pltpu.SMEM), fwd, fwd, fwd, bwd, bwd, bwd],
        out_specs=[fwd, bwd],
        out_shape=[out, out],
        scratch_shapes=[per_head(RET_D, RET_D), per_head(c, c), per_head(c, RET_D), per_head(c, RET_D),
                        per_head(RET_D, RET_D)],
        compiler_params=_params(("parallel", "arbitrary")),
        name="retention",
    )(rld, rq, rk, rv, rq, rk, rv)


def _pair_split(q, mask_a):
    zero = jnp.zeros_like(q)
    return jnp.concatenate([jnp.where(mask_a, q, zero), jnp.where(mask_a, zero, q)], axis=0)


def _dot_t(a, b):
    return lax.dot_general(a, b, (((1,), (1,)), ((), ())), preferred_element_type=F32)


def _na_kernel(idx_ref, q_ref, k0_ref, k1_ref, k2_ref, v0_ref, v1_ref, v2_ref, kc_ref, vc_ref, bias_ref,
               g_ref, o_ref, *, g_units):
    u = ROW_BLOCK
    n_ctx = kc_ref.shape[1]
    g = pl.program_id(1)
    case = jnp.where(g == 0, 0, jnp.where(g < g_units - 1, 1, jnp.where(g == g_units - 1, 2, 3)))
    tile_of = [[idx_ref[case, rho, kp] for kp in range(NA_WIN_UNITS * NA_UNIT_ROWS // 2)]
               for rho in range(NA_UNIT_ROWS)]
    lane = lax.broadcasted_iota(jnp.int32, (1, LANES), 1)
    lo = lane < 64
    pairs = NA_W // LANES
    slices = [slice(j * LANES, (j + 1) * LANES) for j in range(pairs)]
    raw = []
    for sl in slices:
        q = q_ref[0, :, sl]
        k_all = jnp.concatenate([r[0, :, sl] for r in (kc_ref, k0_ref, k1_ref, k2_ref)], axis=0)
        zero = jnp.zeros_like(q)
        raw.append([_dot_t(jnp.where(lo, q, zero), k_all), _dot_t(jnp.where(lo, zero, q), k_all)])
    for j, sl in enumerate(slices):
        v_all = jnp.concatenate([r[0, :, sl] for r in (vc_ref, v0_ref, v1_ref, v2_ref)], axis=0)
        outs = []
        for x in range(2):

            def bias(i):
                return jnp.concatenate(
                    [jnp.concatenate([bias_ref[j, x, tile_of[rho][2 * i]], bias_ref[j, x, tile_of[rho][2 * i + 1]]],
                                     axis=1) for rho in range(NA_UNIT_ROWS)], axis=0)

            s = raw[j][x]
            s = jnp.concatenate([s[:, :n_ctx]] + [s[:, n_ctx + i * u:n_ctx + (i + 1) * u] + bias(i)
                                                  for i in range(NA_WIN_UNITS)], axis=1)
            p = jnp.exp2(s - jnp.max(s, axis=-1, keepdims=True))
            den = jnp.sum(p, axis=-1, keepdims=True)
            outs.append(jnp.dot(p.astype(BF16), v_all, preferred_element_type=F32) / den)
        o = jnp.where(lo, outs[0], outs[1])
        o_ref[0, :, sl] = (o * g_ref[0, :, sl].astype(F32)).astype(BF16)


def _na_bias_tables(rpb, rows):
    ur = NA_UNIT_ROWS
    g_units = rows // ur
    n_dr = 2 * NA_WIN_ROWS - 1
    n_dc = 2 * NA_WIN_COLS - 1
    qc = np.arange(GRID_W)
    cstart = np.clip(qc - NA_WIN_COLS // 2, 0, GRID_W - NA_WIN_COLS)
    col_ok = (qc[None, :] >= cstart[:, None]) & (qc[None, :] < cstart[:, None] + NA_WIN_COLS)
    dc = qc[None, :] - qc[:, None] + NA_WIN_COLS - 1
    hot_c = ((dc[None] == np.arange(n_dc)[:, None, None]) & col_ok[None]).astype(np.float32)
    hi = lax.Precision.HIGHEST
    toep = jnp.einsum('lhrd,dqj->lhrqj', rpb.astype(F32) * LOG2E, jnp.asarray(hot_c), precision=hi)
    toep = jnp.where(jnp.asarray(col_ok), toep, MASKED)
    toep = jnp.concatenate([toep, jnp.full_like(toep[:, :, :1], MASKED)], axis=2)

    combos = {(n_dr, n_dr): 0}
    index = np.zeros((4, ur, NA_WIN_UNITS * ur // 2), np.int32)
    for case, g in enumerate((0, 1, g_units - 1)):
        ws = int(np.clip(g - 1, 0, g_units - NA_WIN_UNITS)) * ur
        for rho in range(ur):
            r = g * ur + rho
            rs = int(np.clip(r - NA_WIN_ROWS // 2, 0, rows - NA_WIN_ROWS))
            for kp in range(NA_WIN_UNITS * ur // 2):
                pair = []
                for kr in (ws + 2 * kp, ws + 2 * kp + 1):
                    ok = rs <= kr < rs + NA_WIN_ROWS
                    pair.append(kr - r + NA_WIN_ROWS - 1 if ok else n_dr)
                index[case, rho, kp] = combos.setdefault(tuple(pair), len(combos))
    left = np.array([c[0] for c in combos], np.int32)
    right = np.array([c[1] for c in combos], np.int32)
    tiles = jnp.concatenate([jnp.take(toep, left, axis=2), jnp.take(toep, right, axis=2)], axis=-1)
    depth = rpb.shape[0]
    tiles = tiles.reshape(depth, NA_HEADS // 2, 2, len(combos), GRID_W, 2 * GRID_W)
    return tiles, jnp.asarray(index)


def _na_call(nq, nk, nv, ng, tiles, index, n_lat, n_ctx, with_ctx):
    b, t, _ = nq.shape
    u = ROW_BLOCK
    g_units = n_lat // u
    n_units = (t if with_ctx else n_lat) // u

    def win(i):
        return lambda bi, g: (bi, jnp.clip(g - 1, 0, g_units - NA_WIN_UNITS) + i, 0)

    tok = pl.BlockSpec((1, u, NA_W), lambda bi, g: (bi, g, 0))
    ctx = pl.BlockSpec((1, n_ctx, NA_W), lambda bi, g: (bi, n_lat // n_ctx, 0))
    kv = [pl.BlockSpec((1, u, NA_W), win(i)) for i in range(NA_WIN_UNITS)]
    return pl.pallas_call(
        functools.partial(_na_kernel, g_units=g_units),
        grid=(b, n_units),
        in_specs=[pl.BlockSpec(memory_space=pltpu.SMEM), tok] + kv + kv
        + [ctx, ctx, pl.BlockSpec(tiles.shape, lambda bi, g: (0,) * tiles.ndim), tok],
        out_specs=tok,
        out_shape=jax.ShapeDtypeStruct((b, n_units * u, NA_W), BF16),
        compiler_params=_params(("parallel", "arbitrary")),
        name="nbr_attn",
    )(index, nq, nk, nk, nk, nv, nv, nv, nk, nv, tiles, ng)


def _gqa_kernel(bound_ref, q_ref, k_ref, v_ref, g_ref, o_ref, q2_sc, m_sc, l_sc, acc_sc, *,
                n_lat, tk_shift, tk_online):
    i = pl.program_id(1)
    tq = q_ref.shape[1]
    pairs = GQA_W // LANES
    lane = lax.broadcasted_iota(jnp.int32, (1, LANES), 1)
    mask_a = (lane % 64) < 32
    lo = lane < 64
    for j in range(pairs):
        q2_sc[j] = _pair_split(q_ref[0, :, j * LANES:(j + 1) * LANES], mask_a)
    t_all = k_ref.shape[1]
    is_latent = i * tq < n_lat
    bound = bound_ref[0]

    def over_keys(step, tk):
        @pl.when(is_latent)
        def _():
            def body(c, carry):
                off = pl.multiple_of(c * tk, tk)
                step(k_ref[0, pl.ds(off, tk), :], v_ref[0, pl.ds(off, tk), :])
                return carry
            lax.fori_loop(0, n_lat // tk, body, 0)
        step(k_ref[0, n_lat:t_all, :], v_ref[0, n_lat:t_all, :])

    def write_out(j, o2):
        sl = slice(j * LANES, (j + 1) * LANES)
        o = jnp.where(lo, o2[:tq], o2[tq:])
        o_ref[0, :, sl] = (o * g_ref[0, :, sl].astype(F32)).astype(BF16)

    @pl.when(bound <= GQA_SHIFT_LIMIT)
    def _():
        l_sc[...] = jnp.zeros_like(l_sc)
        acc_sc[...] = jnp.zeros_like(acc_sc)

        def step(k, v):
            width = k.shape[0]
            for j in range(pairs):
                p = jnp.exp2(_dot_t(q2_sc[j], k) - bound)
                part = p[:, :LANES]
                for cidx in range(1, width // LANES):
                    part = part + p[:, cidx * LANES:(cidx + 1) * LANES]
                l_sc[j] = l_sc[j] + part
                acc_sc[j] = acc_sc[j] + jnp.dot(p.astype(BF16), v, preferred_element_type=F32)

        over_keys(step, tk_shift)
        for j in range(pairs):
            write_out(j, acc_sc[j] / jnp.sum(l_sc[j], axis=-1, keepdims=True))

    @pl.when(bound > GQA_SHIFT_LIMIT)
    def _():
        m_sc[...] = jnp.full_like(m_sc, MASKED)
        l_sc[...] = jnp.zeros_like(l_sc)
        acc_sc[...] = jnp.zeros_like(acc_sc)

        def step(k, v):
            width = k.shape[0]
            for j in range(pairs):
                s = _dot_t(q2_sc[j], k)
                m_old = m_sc[j]
                m_new = jnp.maximum(m_old, jnp.max(s, axis=-1, keepdims=True))
                alpha = jnp.exp2(m_old - m_new)
                p = jnp.exp2(s - jnp.concatenate([m_new] * (width // LANES), axis=1))
                l_sc[j] = alpha * l_sc[j] + jnp.sum(p, axis=-1, keepdims=True)
                acc_sc[j] = alpha * acc_sc[j] + jnp.dot(p.astype(BF16), v, preferred_element_type=F32)
                m_sc[j] = m_new

        over_keys(step, tk_online)
        for j in range(pairs):
            write_out(j, acc_sc[j] / l_sc[j])


def _gqa_score_bound(q_norm_w, k_norm_w):
    bound = HEAD_D * jnp.max(jnp.abs(q_norm_w)) * jnp.max(jnp.abs(k_norm_w)) * (HEAD_D ** -0.5 * LOG2E)
    return (bound * 1.02).reshape(1).astype(F32)


def _gqa_call(bound, gq, gk, gv, gg, n_lat, with_ctx):
    b, t, _ = gq.shape
    tq = ROW_BLOCK
    t_out = t if with_ctx else n_lat
    pairs = GQA_W // LANES
    tok = pl.BlockSpec((1, tq, GQA_W), lambda bi, i: (bi, i, 0))
    kv = pl.BlockSpec((1, t, GQA_KV_W), lambda bi, i: (bi, 0, 0))
    return pl.pallas_call(
        functools.partial(_gqa_kernel, n_lat=n_lat, tk_shift=min(GQA_KEY_CHUNK, n_lat),
                          tk_online=min(GQA_ONLINE_KEY_CHUNK, n_lat)),
        grid=(b, t_out // tq),
        in_specs=[pl.BlockSpec(memory_space=pltpu.SMEM), tok, kv, kv, tok],
        out_specs=tok,
        out_shape=jax.ShapeDtypeStruct((b, t_out, GQA_W), BF16),
        scratch_shapes=[pltpu.VMEM((pairs, 2 * tq, LANES), BF16)] + [pltpu.VMEM((pairs, 2 * tq, LANES), F32)] * 3,
        compiler_params=_params(("parallel", "arbitrary")),
        name="gqa_attn",
    )(bound, gq, gk, gv, gg)


def _merge_kernel(x_ref, mod_ref, of_ref, ob_ref, rg_ref, gnw_ref, na_ref, gqa_ref, h_ref,
                  wm_ref, wr_ref, wn_ref, wg_ref, wo_ref, fw_ref, o_ref, *, final):
    def branch_gate(j):
        return _sigmoid(jnp.dot(h_ref[0], wm_ref[:, j * D_MODEL:(j + 1) * D_MODEL], preferred_element_type=F32))

    o = of_ref[0] + ob_ref[0]
    parts = []
    for h in range(RET_HEADS):
        oh = o[:, h * RET_D:(h + 1) * RET_D]
        mu = jnp.mean(oh, axis=-1, keepdims=True)
        dev = oh - mu
        var = jnp.mean(dev * dev, axis=-1, keepdims=True)
        parts.append(dev * lax.rsqrt(var + EPS))
    ret_x = (jnp.concatenate(parts, axis=-1) * gnw_ref[...] * rg_ref[0].astype(F32)).astype(BF16)
    m = branch_gate(0) * jnp.dot(ret_x, wr_ref[...], preferred_element_type=F32)
    m = m + branch_gate(1) * jnp.dot(na_ref[0], wn_ref[...], preferred_element_type=F32)
    m = m + branch_gate(2) * jnp.dot(gqa_ref[0], wg_ref[...], preferred_element_type=F32)
    out = jnp.dot(m.astype(BF16), wo_ref[...], preferred_element_type=F32)
    gate = mod_ref[0][:, 2 * D_MODEL:]
    xn = x_ref[0] + gate * out
    if final:
        ms = jnp.mean(xn * xn, axis=-1, keepdims=True)
        xn = xn * lax.rsqrt(ms + EPS) * fw_ref[...]
    o_ref[0] = xn


def _merge_call(xa, mod, ret_o, rg, gnw, na_o, gqa_o, h, wm, wr, wn, wg, wo, fw, n_lat, final):
    b, t, _ = xa.shape
    tm = ROW_BLOCK
    t_out = n_lat if final else t
    tok = lambda w: pl.BlockSpec((1, tm, w), lambda bi, i: (bi, i, 0))
    const2 = lambda shape: pl.BlockSpec(shape, lambda bi, i: (0, 0))
    return pl.pallas_call(
        functools.partial(_merge_kernel, final=final),
        grid=(b, t_out // tm),
        in_specs=[
            tok(D_MODEL),
            pl.BlockSpec((1, 1, 3 * D_MODEL), _mod_index(n_lat // tm, b)),
            tok(RET_W),
            tok(RET_W),
            tok(RET_W),
            const2((1, RET_W)),
            tok(NA_W),
            tok(GQA_W),
            tok(D_MODEL),
            const2((D_MODEL, N_BRANCH * D_MODEL)),
            const2((RET_W, D_MODEL)),
            const2((NA_W, D_MODEL)),
            const2((GQA_W, D_MODEL)),
            const2((D_MODEL, D_MODEL)),
            const2((1, D_MODEL)),
        ],
        out_specs=tok(D_MODEL),
        out_shape=jax.ShapeDtypeStruct((b, t_out, D_MODEL), F32),
        compiler_params=_params(("parallel", "parallel")),
        name="merge",
    )(xa, mod, ret_o[0], ret_o[1], rg, gnw, na_o, gqa_o, h, wm, wr, wn, wg, wo, fw)


def _forward(x, c, ctx, c_ctx, ada_w, ada_b, norm_w, w_in, ret_log_decay, ret_gn_w, na_rpb,
             q_norm_w, k_norm_w, w_ret_o, w_na_o, w_gqa_o, w_out, final_norm_w):
    b, n, _ = x.shape
    n_ctx = ctx.shape[1]
    depth = ada_w.shape[0]
    t = n + n_ctx
    rows = n // GRID_W
    assert n % ROW_BLOCK == 0 and n_ctx == ROW_BLOCK and n % n_ctx == 0
    assert n % min(GQA_KEY_CHUNK, n) == 0 and n % min(GQA_ONLINE_KEY_CHUNK, n) == 0
    assert rows >= NA_WIN_UNITS * NA_UNIT_ROWS and rows % NA_UNIT_ROWS == 0

    w_in_p = _permute_w_in(w_in).astype(BF16)
    w_mg_p = w_in[..., O_MG:].astype(BF16)
    qw =_k_layout(jnp.concatenate([q_norm_w, q_norm_w], axis=-1)).reshape(depth, 1, LANES)
    kw = _k_layout(jnp.concatenate([k_norm_w, k_norm_w], axis=-1)).reshape(depth, 1, LANES)
    w_ret_p = w_ret_o.astype(BF16)
    w_na_p = w_na_o.astype(BF16)
    w_gqa_p = _o_layout(w_gqa_o, axis=-2).astype(BF16)
    w_out_p = w_out.astype(BF16)
    na_tiles, na_index = _na_bias_tables(na_rpb, rows)

    cos, sin = _rope_tables(n, t)
    mod_rows = 8 * ((b + 1 + 7) // 8)
    c_rows = jnp.concatenate([c, c_ctx[None, :], jnp.zeros((mod_rows - b - 1, D_MODEL), F32)], axis=0)
    mod_all = _ada_call(c_rows, ada_w, ada_b)

    xa = jnp.concatenate([x, ctx], axis=1)
    for layer in range(depth):
        final = layer == depth - 1
        mod = mod_all[layer].reshape(mod_rows, 1, 3 * D_MODEL)
        (rq, rk, rv, rg, nq, nk, nv, ng, gq, gk, gv, gg, h) = _inproj_call(
            xa, mod, norm_w[layer].reshape(1, D_MODEL), w_in_p[layer], cos, sin, qw[layer], kw[layer], n)
        ret_o = _ret_call(ret_log_decay[layer], rq, rk, rv, n)
        na_o = _na_call(nq, nk, nv, ng, na_tiles[layer], na_index, n, n_ctx, with_ctx=not final)
        gqa_o = _gqa_call(_gqa_score_bound(q_norm_w[layer], k_norm_w[layer]), gq, gk, gv, gg, n,
                          with_ctx=not final)
        xa = _merge_call(xa, mod, ret_o, rg, ret_gn_w[layer].reshape(1, RET_W), na_o, gqa_o, h,
                         w_mg_p[layer], w_ret_p[layer], w_na_p[layer], w_gqa_p[layer], w_out_p[layer],
                         final_norm_w.reshape(1, D_MODEL), n, final)
    return xa


def kernel(x, c, ctx, c_ctx, ada_w, ada_b, norm_w, w_in, ret_log_decay, ret_gn_w, na_rpb, q_norm_w, k_norm_w,
           w_ret_o, w_na_o, w_gqa_o, w_out, final_norm_w):
    return _forward(x, c, ctx, c_ctx, ada_w, ada_b, norm_w, w_in, ret_log_decay, ret_gn_w, na_rpb,
                    q_norm_w, k_norm_w, w_ret_o, w_na_o, w_gqa_o, w_out, final_norm_w)
```

```python
import functools

import numpy as np
import jax
import jax.numpy as jnp
from jax import lax
from jax.experimental import pallas as pl
from jax.experimental.pallas import tpu as pltpu

F32 = jnp.float32
BF16 = jnp.bfloat16

D_MODEL = 1024
GRID_W = 64
EPS = 1e-6
ROPE_THETA = 10000.0
RET_HEADS = 4
RET_D = 128
RET_W = RET_HEADS * RET_D
RET_STEP = 256
NA_HEADS = 8
HEAD_D = 64
NA_W = NA_HEADS * HEAD_D
NA_WIN_ROWS = 8
NA_WIN_COLS = 16
GQA_HEADS = 8
GQA_KV_HEADS = 2
GQA_W = GQA_HEADS * HEAD_D
GQA_KV_W = GQA_KV_HEADS * HEAD_D
N_BRANCH = 3
LANES = 128
MXU_COLS = 256
ROW_BLOCK = 256
NA_UNIT_ROWS = ROW_BLOCK // GRID_W
NA_WIN_UNITS = 3
GQA_KEY_CHUNK = 4096
GQA_ONLINE_KEY_CHUNK = 2048
LOG2E = 1.4426950408889634
GQA_SHIFT_LIMIT = 40.0
MASKED = -1e30
VMEM_LIMIT = 56 * 1024 * 1024

_SPLITS = (RET_W, RET_W, RET_W, RET_W, NA_W, NA_W, NA_W, NA_W, GQA_W, GQA_KV_W, GQA_KV_W, GQA_W,
           N_BRANCH * D_MODEL)
_OFFS = np.concatenate([[0], np.cumsum(_SPLITS)])
IN_COLS = int(_OFFS[-1])
(O_RQ, O_RK, O_RV, O_RG, O_NQ, O_NK, O_NV, O_NG, O_GQ, O_GK, O_GV, O_GG, O_MG) = (int(o) for o in _OFFS[:-1])


def _params(sem, vmem=VMEM_LIMIT):
    return pltpu.CompilerParams(dimension_semantics=sem, vmem_limit_bytes=vmem)


def _silu(x):
    return x / (1.0 + jnp.exp(-x))


def _sigmoid(x):
    return 1.0 / (1.0 + jnp.exp(-x))


def _lane_dims():
    l = np.arange(LANES)
    half = l // 64
    slot = (l % 64) // 32
    r = l % 32
    d = np.where(r < 16, r, 32 + r - 16) + 16 * half
    return slot, d


def _transpose_cols(w, shape, order, axis=-1):
    axis = axis % w.ndim
    k = len(shape)
    full = w.shape[:axis] + tuple(shape) + w.shape[axis + 1:]
    perm = tuple(range(axis)) + tuple(axis + o for o in order) + tuple(range(axis + k, len(full)))
    return jnp.transpose(w.reshape(full), perm).reshape(w.shape)


def _q_layout(w):
    return _transpose_cols(w, (2, GQA_HEADS // 2, 2, 2, 16), (1, 3, 0, 2, 4))


def _k_layout(w):
    return _transpose_cols(w, (2, 2, 2, 16), (2, 0, 1, 3))


def _o_layout(w, axis=-1):
    return _transpose_cols(w, (2, GQA_HEADS // 2, HEAD_D), (1, 0, 2), axis=axis)


def _permute_w_in(w_in):
    parts = [w_in[..., :O_GQ], _q_layout(w_in[..., O_GQ:O_GK]), _k_layout(w_in[..., O_GK:O_GV]),
             w_in[..., O_GV:O_GG], _o_layout(w_in[..., O_GG:O_MG])]
    return jnp.concatenate(parts, axis=-1)


def _rope_tables(n, t_total):
    _, d = _lane_dims()
    quarter = HEAD_D // 4
    freqs = ROPE_THETA ** (-jnp.arange(quarter, dtype=F32) / quarter)
    t = jnp.arange(n)
    pos_r = (t // GRID_W).astype(F32)
    pos_c = (t % GRID_W).astype(F32)
    dd = d % 32
    fi = dd % quarter
    use_col = (d // 32) == 1
    first = dd < quarter
    ang = jnp.where(use_col[None, :], pos_c[:, None], pos_r[:, None]) * freqs[fi][None, :]
    cos = jnp.cos(ang)
    sin = jnp.where(first[None, :], -jnp.sin(ang), jnp.sin(ang))
    pad = t_total - n
    cos = jnp.concatenate([cos, jnp.ones((pad, LANES), F32)], axis=0)
    sin = jnp.concatenate([sin, jnp.zeros((pad, LANES), F32)], axis=0)
    return cos, sin


def _ada_kernel(c_ref, w_ref, b_ref, o_ref):
    cs = _silu(c_ref[...])
    o_ref[0] = jnp.dot(cs, w_ref[0], preferred_element_type=F32) + b_ref[0]


def _ada_call(c_rows, ada_w, ada_b):
    depth = ada_w.shape[0]
    rows = c_rows.shape[0]
    nb = 3
    return pl.pallas_call(
        _ada_kernel,
        grid=(depth, nb),
        in_specs=[
            pl.BlockSpec((rows, D_MODEL), lambda l, j: (0, 0)),
            pl.BlockSpec((1, D_MODEL, D_MODEL), lambda l, j: (l, 0, j)),
            pl.BlockSpec((1, 1, D_MODEL), lambda l, j: (l, 0, j)),
        ],
        out_specs=pl.BlockSpec((1, rows, D_MODEL), lambda l, j: (l, 0, j)),
        out_shape=jax.ShapeDtypeStruct((depth, rows, nb * D_MODEL), F32),
        compiler_params=_params(("parallel", "parallel")),
        name="ada_mod",
    )(c_rows, ada_w, ada_b.reshape(depth, 1, nb * D_MODEL))


def _head_mean_sq(t, mask_a):
    sq = t * t
    s_all = jnp.sum(sq, axis=-1, keepdims=True)
    s_a = jnp.sum(jnp.where(mask_a, sq, 0.0), axis=-1, keepdims=True)
    return jnp.where(mask_a, s_a, s_all - s_a) * (1.0 / HEAD_D)


def _norm_rope(t, w_lane, cos, sin, mask_a):
    tn = t * lax.rsqrt(_head_mean_sq(t, mask_a) + EPS) * w_lane
    return tn * cos + pltpu.roll(tn, LANES // 2, 1) * sin


def _inproj_kernel(x_ref, mod_ref, nw_ref, w_ref, cos_ref, sin_ref, qw_ref, kw_ref,
                   rq_ref, rk_ref, rv_ref, rg_ref, nq_ref, nk_ref, nv_ref, ng_ref,
                   gq_ref, gk_ref, gv_ref, gg_ref, h_ref):
    x = x_ref[0]
    ms = jnp.mean(x * x, axis=-1, keepdims=True)
    y = x * lax.rsqrt(ms + EPS) * nw_ref[...]
    mod = mod_ref[0]
    shift = mod[:, :D_MODEL]
    scale = mod[:, D_MODEL:2 * D_MODEL]
    h = (y * (1.0 + scale) + shift).astype(BF16)

    def proj(off, width):
        return jnp.dot(h, w_ref[:, off:off + width], preferred_element_type=F32)

    qk_scale = HEAD_D ** -0.5
    rq_ref[0] = proj(O_RQ, RET_W).astype(BF16)
    rk_ref[0] = proj(O_RK, RET_W).astype(BF16)
    rv_ref[0] = proj(O_RV, RET_W).astype(BF16)
    rg_ref[0] = _silu(proj(O_RG, RET_W)).astype(BF16)
    nq_ref[0] = (proj(O_NQ, NA_W) * (qk_scale * LOG2E)).astype(BF16)
    nk_ref[0] = proj(O_NK, NA_W).astype(BF16)
    nv_ref[0] = proj(O_NV, NA_W).astype(BF16)
    ng_ref[0] = _silu(proj(O_NG, NA_W)).astype(BF16)

    lane = lax.broadcasted_iota(jnp.int32, (1, LANES), 1)
    mask_a = (lane % 64) < 32
    cos = cos_ref[...]
    sin = sin_ref[...]
    qw = qw_ref[...]
    for j2 in range(GQA_W // MXU_COLS):
        t2 = proj(O_GQ + j2 * MXU_COLS, MXU_COLS)
        for j in range(2 * j2, 2 * j2 + 2):
            t = t2[:, (j % 2) * LANES:(j % 2 + 1) * LANES]
            gq_ref[0, :, j * LANES:(j + 1) * LANES] = (_norm_rope(t, qw, cos, sin, mask_a)
                                                       * (qk_scale * LOG2E)).astype(BF16)
    kv = proj(O_GK, 2 * GQA_KV_W)
    gk_ref[0] = _norm_rope(kv[:, :GQA_KV_W], kw_ref[...], cos, sin, mask_a).astype(BF16)
    gv_ref[0] = kv[:, GQA_KV_W:].astype(BF16)
    gg_ref[0] = _silu(proj(O_GG, GQA_W)).astype(BF16)
    h_ref[0] = h


def _mod_index(n_lat_blocks, batch):
    return lambda b, i: (jnp.where(i < n_lat_blocks, b, batch), 0, 0)


def _inproj_call(xa, mod, norm_w, w_in, cos, sin, qw, kw, n_lat):
    b, t, _ = xa.shape
    tm = ROW_BLOCK
    widths = (RET_W, RET_W, RET_W, RET_W, NA_W, NA_W, NA_W, NA_W, GQA_W, GQA_KV_W, GQA_KV_W, GQA_W, D_MODEL)
    tok = lambda w: pl.BlockSpec((1, tm, w), lambda bi, i: (bi, i, 0))
    const2 = lambda shape: pl.BlockSpec(shape, lambda bi, i: (0, 0))
    return pl.pallas_call(
        _inproj_kernel,
        grid=(b, t // tm),
        in_specs=[
            tok(D_MODEL),
            pl.BlockSpec((1, 1, 3 * D_MODEL), _mod_index(n_lat // tm, b)),
            const2((1, D_MODEL)),
            pl.BlockSpec((D_MODEL, O_MG), lambda bi, i: (0, 0), pipeline_mode=pl.Buffered(1)),
            pl.BlockSpec((tm, LANES), lambda bi, i: (i, 0)),
            pl.BlockSpec((tm, LANES), lambda bi, i: (i, 0)),
            const2((1, LANES)),
            const2((1, LANES)),
        ],
        out_specs=[tok(w) for w in widths],
        out_shape=[jax.ShapeDtypeStruct((b, t, w), BF16) for w in widths],
        compiler_params=_params(("parallel", "parallel")),
        name="in_proj",
    )(xa, mod, norm_w, w_in, cos, sin, qw, kw)


def _ret_kernel(rld_ref, qf_ref, kf_ref, vf_ref, qb_ref, kb_ref, vb_ref, of_ref, ob_ref,
                state, dec, xi, zeta, gch):
    s = pl.program_id(1)
    c = RET_STEP
    ksc = RET_D ** -0.5

    @pl.when(s == 0)
    def _():
        state[...] = jnp.zeros_like(state)
        ii = lax.broadcasted_iota(jnp.int32, (c, c), 0)
        jj = lax.broadcasted_iota(jnp.int32, (c, c), 1)
        ri = lax.broadcasted_iota(jnp.int32, (c, RET_D), 0)
        for d in range(2):
            delta = ((ii - jj) if d == 0 else (jj - ii)).astype(F32)
            ip = (ri if d == 0 else c - 1 - ri).astype(F32)
            for h in range(RET_HEADS):
                log_g = -jnp.exp(jnp.full((1, 1), rld_ref[d, h], F32))
                dec[d, h] = jnp.where(delta >= 0, jnp.exp(jnp.maximum(delta, 0.0) * log_g), 0.0) * ksc
                xi[d, h] = jnp.exp((ip + 1.0) * log_g)
                zeta[d, h] = jnp.exp((c - 1.0 - ip) * log_g) * ksc
                gch[d, h] = jnp.exp(jnp.full((RET_D, RET_D), float(c), F32) * log_g)

    for d, (q_ref, k_ref, v_ref, o_ref) in enumerate(((qf_ref, kf_ref, vf_ref, of_ref),
                                                      (qb_ref, kb_ref, vb_ref, ob_ref))):
        for h in range(RET_HEADS):
            sl = slice(h * RET_D, (h + 1) * RET_D)
            q = q_ref[0, :, sl]
            k = k_ref[0, :, sl]
            v = v_ref[0, :, sl]
            scores = lax.dot_general(q, k, (((1,), (1,)), ((), ())), preferred_element_type=F32)
            intra = jnp.dot((scores * dec[d, h]).astype(BF16), v, preferred_element_type=F32)
            st = state[d, h]
            inter = jnp.dot(q, st.astype(BF16), preferred_element_type=F32) * xi[d, h]
            o_ref[0, :, sl] = intra + inter
            vz = (v.astype(F32) * zeta[d, h]).astype(BF16)
            upd = lax.dot_general(k, vz, (((0,), (0,)), ((), ())), preferred_element_type=F32)
            state[d, h] = gch[d, h] * st + upd


def _ret_call(rld, rq, rk, rv, n_lat):
    b, t, _ = rq.shape
    c = RET_STEP
    nc = t // c
    ncn = n_lat // c
    fwd = pl.BlockSpec((1, c, RET_W), lambda bi, s: (bi, (s + ncn) % nc, 0))
    bwd = pl.BlockSpec((1, c, RET_W), lambda bi, s: (bi, nc - 1 - s, 0))
    per_head = lambda rows, cols: pltpu.VMEM((2, RET_HEADS, rows, cols), F32)
    out = jax.ShapeDtypeStruct((b, t, RET_W), F32)
    return pl.pallas_call(
        _ret_kernel,
        grid=(b, nc),
        in_specs=[pl.BlockSpec(memory_space=pltpu.SMEM), fwd, fwd, fwd, bwd, bwd, bwd],
        out_specs=[fwd, bwd],
        out_shape=[out, out],
        scratch_shapes=[per_head(RET_D, RET_D), per_head(c, c), per_head(c, RET_D), per_head(c, RET_D),
                        per_head(RET_D, RET_D)],
        compiler_params=_params(("parallel", "arbitrary")),
        name="retention",
    )(rld, rq, rk, rv, rq, rk, rv)


def _pair_split(q, mask_a):
    zero = jnp.zeros_like(q)
    return jnp.concatenate([jnp.where(mask_a, q, zero), jnp.where(mask_a, zero, q)], axis=0)


def _dot_t(a, b):
    return lax.dot_general(a, b, (((1,), (1,)), ((), ())), preferred_element_type=F32)


def _na_kernel(idx_ref, q_ref, k0_ref, k1_ref, k2_ref, v0_ref, v1_ref, v2_ref, kc_ref, vc_ref, bias_ref,
               g_ref, o_ref, *, g_units):
    u = ROW_BLOCK
    n_ctx = kc_ref.shape[1]
    g = pl.program_id(1)
    case = jnp.where(g == 0, 0, jnp.where(g < g_units - 1, 1, jnp.where(g == g_units - 1, 2, 3)))
    tile_of = [[idx_ref[case, rho, kp] for kp in range(NA_WIN_UNITS * NA_UNIT_ROWS // 2)]
               for rho in range(NA_UNIT_ROWS)]
    lane = lax.broadcasted_iota(jnp.int32, (1, LANES), 1)
    lo = lane < 64
    pairs = NA_W // LANES
    slices = [slice(j * LANES, (j + 1) * LANES) for j in range(pairs)]
    raw = []
    for sl in slices:
        q = q_ref[0, :, sl]
        k_all = jnp.concatenate([r[0, :, sl] for r in (kc_ref, k0_ref, k1_ref, k2_ref)], axis=0)
        zero = jnp.zeros_like(q)
        raw.append([_dot_t(jnp.where(lo, q, zero), k_all), _dot_t(jnp.where(lo, zero, q), k_all)])
    for j, sl in enumerate(slices):
        v_all = jnp.concatenate([r[0, :, sl] for r in (vc_ref, v0_ref, v1_ref, v2_ref)], axis=0)
        outs = []
        for x in range(2):

            def bias(i):
                return jnp.concatenate(
                    [jnp.concatenate([bias_ref[j, x, tile_of[rho][2 * i]], bias_ref[j, x, tile_of[rho][2 * i + 1]]],
                                     axis=1) for rho in range(NA_UNIT_ROWS)], axis=0)

            s = raw[j][x]
            s = jnp.concatenate([s[:, :n_ctx]] + [s[:, n_ctx + i * u:n_ctx + (i + 1) * u] + bias(i)
                                                  for i in range(NA_WIN_UNITS)], axis=1)
            p = jnp.exp2(s - jnp.max(s, axis=-1, keepdims=True))
            den = jnp.sum(p, axis=-1, keepdims=True)
            outs.append(jnp.dot(p.astype(BF16), v_all, preferred_element_type=F32) / den)
        o = jnp.where(lo, outs[0], outs[1])
        o_ref[0, :, sl] = (o * g_ref[0, :, sl].astype(F32)).astype(BF16)


def _na_bias_tables(rpb, rows):
    ur = NA_UNIT_ROWS
    g_units = rows // ur
    n_dr = 2 * NA_WIN_ROWS - 1
    n_dc = 2 * NA_WIN_COLS - 1
    qc = np.arange(GRID_W)
    cstart = np.clip(qc - NA_WIN_COLS // 2, 0, GRID_W - NA_WIN_COLS)
    col_ok = (qc[None, :] >= cstart[:, None]) & (qc[None, :] < cstart[:, None] + NA_WIN_COLS)
    dc = qc[None, :] - qc[:, None] + NA_WIN_COLS - 1
    hot_c = ((dc[None] == np.arange(n_dc)[:, None, None]) & col_ok[None]).astype(np.float32)
    hi = lax.Precision.HIGHEST
    toep = jnp.einsum('lhrd,dqj->lhrqj', rpb.astype(F32) * LOG2E, jnp.asarray(hot_c), precision=hi)
    toep = jnp.where(jnp.asarray(col_ok), toep, MASKED)
    toep = jnp.concatenate([toep, jnp.full_like(toep[:, :, :1], MASKED)], axis=2)

    combos = {(n_dr, n_dr): 0}
    index = np.zeros((4, ur, NA_WIN_UNITS * ur // 2), np.int32)
    for case, g in enumerate((0, 1, g_units - 1)):
        ws = int(np.clip(g - 1, 0, g_units - NA_WIN_UNITS)) * ur
        for rho in range(ur):
            r = g * ur + rho
            rs = int(np.clip(r - NA_WIN_ROWS // 2, 0, rows - NA_WIN_ROWS))
            for kp in range(NA_WIN_UNITS * ur // 2):
                pair = []
                for kr in (ws + 2 * kp, ws + 2 * kp + 1):
                    ok = rs <= kr < rs + NA_WIN_ROWS
                    pair.append(kr - r + NA_WIN_ROWS - 1 if ok else n_dr)
                index[case, rho, kp] = combos.setdefault(tuple(pair), len(combos))
    left = np.array([c[0] for c in combos], np.int32)
    right = np.array([c[1] for c in combos], np.int32)
    tiles = jnp.concatenate([jnp.take(toep, left, axis=2), jnp.take(toep, right, axis=2)], axis=-1)
    depth = rpb.shape[0]
    tiles = tiles.reshape(depth, NA_HEADS // 2, 2, len(combos), GRID_W, 2 * GRID_W)
    return tiles, jnp.asarray(index)


def _na_call(nq, nk, nv, ng, tiles, index, n_lat, n_ctx, with_ctx):
    b, t, _ = nq.shape
    u = ROW_BLOCK
    g_units = n_lat // u
    n_units = (t if with_ctx else n_lat) // u

    def win(i):
        return lambda bi, g: (bi, jnp.clip(g - 1, 0, g_units - NA_WIN_UNITS) + i, 0)

    tok = pl.BlockSpec((1, u, NA_W), lambda bi, g: (bi, g, 0))
    ctx = pl.BlockSpec((1, n_ctx, NA_W), lambda bi, g: (bi, n_lat // n_ctx, 0))
    kv = [pl.BlockSpec((1, u, NA_W), win(i)) for i in range(NA_WIN_UNITS)]
    return pl.pallas_call(
        functools.partial(_na_kernel, g_units=g_units),
        grid=(b, n_units),
        in_specs=[pl.BlockSpec(memory_space=pltpu.SMEM), tok] + kv + kv
        + [ctx, ctx, pl.BlockSpec(tiles.shape, lambda bi, g: (0,) * tiles.ndim), tok],
        out_specs=tok,
        out_shape=jax.ShapeDtypeStruct((b, n_units * u, NA_W), BF16),
        compiler_params=_params(("parallel", "arbitrary")),
        name="nbr_attn",
    )(index, nq, nk, nk, nk, nv, nv, nv, nk, nv, tiles, ng)


def _gqa_kernel(bound_ref, q_ref, k_ref, v_ref, g_ref, o_ref, q2_sc, m_sc, l_sc, acc_sc, *,
                n_lat, tk_shift, tk_online):
    i = pl.program_id(1)
    tq = q_ref.shape[1]
    pairs = GQA_W // LANES
    lane = lax.broadcasted_iota(jnp.int32, (1, LANES), 1)
    mask_a = (lane % 64) < 32
    lo = lane < 64
    for j in range(pairs):
        q2_sc[j] = _pair_split(q_ref[0, :, j * LANES:(j + 1) * LANES], mask_a)
    t_all = k_ref.shape[1]
    is_latent = i * tq < n_lat
    bound = bound_ref[0]

    def over_keys(step, tk):
        @pl.when(is_latent)
        def _():
            def body(c, carry):
                off = pl.multiple_of(c * tk, tk)
                step(k_ref[0, pl.ds(off, tk), :], v_ref[0, pl.ds(off, tk), :])
                return carry
            lax.fori_loop(0, n_lat // tk, body, 0)
        step(k_ref[0, n_lat:t_all, :], v_ref[0, n_lat:t_all, :])

    def write_out(j, o2):
        sl = slice(j * LANES, (j + 1) * LANES)
        o = jnp.where(lo, o2[:tq], o2[tq:])
        o_ref[0, :, sl] = (o * g_ref[0, :, sl].astype(F32)).astype(BF16)

    @pl.when(bound <= GQA_SHIFT_LIMIT)
    def _():
        l_sc[...] = jnp.zeros_like(l_sc)
        acc_sc[...] = jnp.zeros_like(acc_sc)

        def step(k, v):
            width = k.shape[0]
            for j in range(pairs):
                p = jnp.exp2(_dot_t(q2_sc[j], k) - bound)
                part = p[:, :LANES]
                for cidx in range(1, width // LANES):
                    part = part + p[:, cidx * LANES:(cidx + 1) * LANES]
                l_sc[j] = l_sc[j] + part
                acc_sc[j] = acc_sc[j] + jnp.dot(p.astype(BF16), v, preferred_element_type=F32)

        over_keys(step, tk_shift)
        for j in range(pairs):
            write_out(j, acc_sc[j] / jnp.sum(l_sc[j], axis=-1, keepdims=True))

    @pl.when(jnp.logical_not(bound <= GQA_SHIFT_LIMIT))
    def _():
        m_sc[...] = jnp.full_like(m_sc, MASKED)
        l_sc[...] = jnp.zeros_like(l_sc)
        acc_sc[...] = jnp.zeros_like(acc_sc)

        def step(k, v):
            width = k.shape[0]
            for j in range(pairs):
                s = _dot_t(q2_sc[j], k)
                m_old = m_sc[j]
                m_new = jnp.maximum(m_old, jnp.max(s, axis=-1, keepdims=True))
                alpha = jnp.exp2(m_old - m_new)
                p = jnp.exp2(s - jnp.concatenate([m_new] * (width // LANES), axis=1))
                l_sc[j] = alpha * l_sc[j] + jnp.sum(p, axis=-1, keepdims=True)
                acc_sc[j] = alpha * acc_sc[j] + jnp.dot(p.astype(BF16), v, preferred_element_type=F32)
                m_sc[j] = m_new

        over_keys(step, tk_online)
        for j in range(pairs):
            write_out(j, acc_sc[j] / l_sc[j])


def _gqa_score_bound(q_norm_w, k_norm_w):
    bound = HEAD_D * jnp.max(jnp.abs(q_norm_w)) * jnp.max(jnp.abs(k_norm_w)) * (HEAD_D ** -0.5 * LOG2E)
    return (bound * 1.02).reshape(1).astype(F32)


def _gqa_call(bound, gq, gk, gv, gg, n_lat, with_ctx):
    b, t, _ = gq.shape
    tq = ROW_BLOCK
    t_out = t if with_ctx else n_lat
    pairs = GQA_W // LANES
    tok = pl.BlockSpec((1, tq, GQA_W), lambda bi, i: (bi, i, 0))
    kv = pl.BlockSpec((1, t, GQA_KV_W), lambda bi, i: (bi, 0, 0))
    return pl.pallas_call(
        functools.partial(_gqa_kernel, n_lat=n_lat, tk_shift=min(GQA_KEY_CHUNK, n_lat),
                          tk_online=min(GQA_ONLINE_KEY_CHUNK, n_lat)),
        grid=(b, t_out // tq),
        in_specs=[pl.BlockSpec(memory_space=pltpu.SMEM), tok, kv, kv, tok],
        out_specs=tok,
        out_shape=jax.ShapeDtypeStruct((b, t_out, GQA_W), BF16),
        scratch_shapes=[pltpu.VMEM((pairs, 2 * tq, LANES), BF16)] + [pltpu.VMEM((pairs, 2 * tq, LANES), F32)] * 3,
        compiler_params=_params(("parallel", "arbitrary")),
        name="gqa_attn",
    )(bound, gq, gk, gv, gg)


def _merge_kernel(x_ref, mod_ref, of_ref, ob_ref, rg_ref, gnw_ref, na_ref, gqa_ref, h_ref,
                  wm_ref, wr_ref, wn_ref, wg_ref, wo_ref, fw_ref, o_ref, *, final):
    def branch_gate(j):
        return _sigmoid(jnp.dot(h_ref[0], wm_ref[:, j * D_MODEL:(j + 1) * D_MODEL], preferred_element_type=F32))

    o = of_ref[0] + ob_ref[0]
    parts = []
    for h in range(RET_HEADS):
        oh = o[:, h * RET_D:(h + 1) * RET_D]
        mu = jnp.mean(oh, axis=-1, keepdims=True)
        dev = oh - mu
        var = jnp.mean(dev * dev, axis=-1, keepdims=True)
        parts.append(dev * lax.rsqrt(var + EPS))
    ret_x = (jnp.concatenate(parts, axis=-1) * gnw_ref[...] * rg_ref[0].astype(F32)).astype(BF16)
    m = branch_gate(0) * jnp.dot(ret_x, wr_ref[...], preferred_element_type=F32)
    m = m + branch_gate(1) * jnp.dot(na_ref[0], wn_ref[...], preferred_element_type=F32)
    m = m + branch_gate(2) * jnp.dot(gqa_ref[0], wg_ref[...], preferred_element_type=F32)
    out = jnp.dot(m.astype(BF16), wo_ref[...], preferred_element_type=F32)
    gate = mod_ref[0][:, 2 * D_MODEL:]
    xn = x_ref[0] + gate * out
    if final:
        ms = jnp.mean(xn * xn, axis=-1, keepdims=True)
        xn = xn * lax.rsqrt(ms + EPS) * fw_ref[...]
    o_ref[0] = xn


def _merge_call(xa, mod, ret_o, rg, gnw, na_o, gqa_o, h, wm, wr, wn, wg, wo, fw, n_lat, final):
    b, t, _ = xa.shape
    tm = ROW_BLOCK
    t_out = n_lat if final else t
    tok = lambda w: pl.BlockSpec((1, tm, w), lambda bi, i: (bi, i, 0))
    const2 = lambda shape: pl.BlockSpec(shape, lambda bi, i: (0, 0))
    return pl.pallas_call(
        functools.partial(_merge_kernel, final=final),
        grid=(b, t_out // tm),
        in_specs=[
            tok(D_MODEL),
            pl.BlockSpec((1, 1, 3 * D_MODEL), _mod_index(n_lat // tm, b)),
            tok(RET_W),
            tok(RET_W),
            tok(RET_W),
            const2((1, RET_W)),
            tok(NA_W),
            tok(GQA_W),
            tok(D_MODEL),
            const2((D_MODEL, N_BRANCH * D_MODEL)),
            const2((RET_W, D_MODEL)),
            const2((NA_W, D_MODEL)),
            const2((GQA_W, D_MODEL)),
            const2((D_MODEL, D_MODEL)),
            const2((1, D_MODEL)),
        ],
        out_specs=tok(D_MODEL),
        out_shape=jax.ShapeDtypeStruct((b, t_out, D_MODEL), F32),
        compiler_params=_params(("parallel", "parallel")),
        name="merge",
    )(xa, mod, ret_o[0], ret_o[1], rg, gnw, na_o, gqa_o, h, wm, wr, wn, wg, wo, fw)


def _forward(x, c, ctx, c_ctx, ada_w, ada_b, norm_w, w_in, ret_log_decay, ret_gn_w, na_rpb,
             q_norm_w, k_norm_w, w_ret_o, w_na_o, w_gqa_o, w_out, final_norm_w):
    b, n, _ = x.shape
    n_ctx = ctx.shape[1]
    depth = ada_w.shape[0]
    t = n + n_ctx
    rows = n // GRID_W
    assert n % ROW_BLOCK == 0 and n_ctx == ROW_BLOCK and n % n_ctx == 0
    assert n % min(GQA_KEY_CHUNK, n) == 0 and n % min(GQA_ONLINE_KEY_CHUNK, n) == 0
    assert rows >= NA_WIN_UNITS * NA_UNIT_ROWS and rows % NA_UNIT_ROWS == 0

    w_in_p = _permute_w_in(w_in).astype(BF16)
    w_mg_p = w_in[..., O_MG:].astype(BF16)
    qw =_k_layout(jnp.concatenate([q_norm_w, q_norm_w], axis=-1)).reshape(depth, 1, LANES)
    kw = _k_layout(jnp.concatenate([k_norm_w, k_norm_w], axis=-1)).reshape(depth, 1, LANES)
    w_ret_p = w_ret_o.astype(BF16)
    w_na_p = w_na_o.astype(BF16)
    w_gqa_p = _o_layout(w_gqa_o, axis=-2).astype(BF16)
    w_out_p = w_out.astype(BF16)
    na_tiles, na_index = _na_bias_tables(na_rpb, rows)

    cos, sin = _rope_tables(n, t)
    mod_rows = 8 * ((b + 1 + 7) // 8)
    c_rows = jnp.concatenate([c, c_ctx[None, :], jnp.zeros((mod_rows - b - 1, D_MODEL), F32)], axis=0)
    mod_all = _ada_call(c_rows, ada_w, ada_b)

    xa = jnp.concatenate([x, ctx], axis=1)
    for layer in range(depth):
        final = layer == depth - 1
        mod = mod_all[layer].reshape(mod_rows, 1, 3 * D_MODEL)
        (rq, rk, rv, rg, nq, nk, nv, ng, gq, gk, gv, gg, h) = _inproj_call(
            xa, mod, norm_w[layer].reshape(1, D_MODEL), w_in_p[layer], cos, sin, qw[layer], kw[layer], n)
        ret_o = _ret_call(ret_log_decay[layer], rq, rk, rv, n)
        na_o = _na_call(nq, nk, nv, ng, na_tiles[layer], na_index, n, n_ctx, with_ctx=not final)
        gqa_o = _gqa_call(_gqa_score_bound(q_norm_w[layer], k_norm_w[layer]), gq, gk, gv, gg, n,
                          with_ctx=not final)
        xa = _merge_call(xa, mod, ret_o, rg, ret_gn_w[layer].reshape(1, RET_W), na_o, gqa_o, h,
                         w_mg_p[layer], w_ret_p[layer], w_na_p[layer], w_gqa_p[layer], w_out_p[layer],
                         final_norm_w.reshape(1, D_MODEL), n, final)
    return xa


def kernel(x, c, ctx, c_ctx, ada_w, ada_b, norm_w, w_in, ret_log_decay, ret_gn_w, na_rpb, q_norm_w, k_norm_w,
           w_ret_o, w_na_o, w_gqa_o, w_out, final_norm_w):
    return _forward(x, c, ctx, c_ctx, ada_w, ada_b, norm_w, w_in, ret_log_decay, ret_gn_w, na_rpb,
                    q_norm_w, k_norm_w, w_ret_o, w_na_o, w_gqa_o, w_out, final_norm_w)
```

```python
import functools

import numpy as np
import jax
import jax.numpy as jnp
from jax import lax
from jax.experimental import pallas as pl
from jax.experimental.pallas import tpu as pltpu

F32 = jnp.float32
BF16 = jnp.bfloat16

D_MODEL = 1024
GRID_W = 64
EPS = 1e-6
ROPE_THETA = 10000.0
RET_HEADS = 4
RET_D = 128
RET_W = RET_HEADS * RET_D
RET_STEP = 256
NA_HEADS = 8
HEAD_D = 64
NA_W = NA_HEADS * HEAD_D
NA_WIN_ROWS = 8
NA_WIN_COLS = 16
GQA_HEADS = 8
GQA_KV_HEADS = 2
GQA_W = GQA_HEADS * HEAD_D
GQA_KV_W = GQA_KV_HEADS * HEAD_D
N_BRANCH = 3
LANES = 128
MXU_COLS = 256
ROW_BLOCK = 256
NA_UNIT_ROWS = ROW_BLOCK // GRID_W
NA_WIN_UNITS = 3
GQA_KEY_CHUNK = 4096
GQA_ONLINE_KEY_CHUNK = 2048
LOG2E = 1.4426950408889634
GQA_SHIFT_LIMIT = 40.0
MASKED = -1e30
VMEM_LIMIT = 56 * 1024 * 1024

_SPLITS = (RET_W, RET_W, RET_W, RET_W, NA_W, NA_W, NA_W, NA_W, GQA_W, GQA_KV_W, GQA_KV_W, GQA_W,
           N_BRANCH * D_MODEL)
_OFFS = np.concatenate([[0], np.cumsum(_SPLITS)])
IN_COLS = int(_OFFS[-1])
(O_RQ, O_RK, O_RV, O_RG, O_NQ, O_NK, O_NV, O_NG, O_GQ, O_GK, O_GV, O_GG, O_MG) = (int(o) for o in _OFFS[:-1])


def _params(sem, vmem=VMEM_LIMIT):
    return pltpu.CompilerParams(dimension_semantics=sem, vmem_limit_bytes=vmem)


def _silu(x):
    return x / (1.0 + jnp.exp(-x))


def _sigmoid(x):
    return 1.0 / (1.0 + jnp.exp(-x))


def _lane_dims():
    l = np.arange(LANES)
    half = l // 64
    slot = (l % 64) // 32
    r = l % 32
    d = np.where(r < 16, r, 32 + r - 16) + 16 * half
    return slot, d


def _transpose_cols(w, shape, order, axis=-1):
    axis = axis % w.ndim
    k = len(shape)
    full = w.shape[:axis] + tuple(shape) + w.shape[axis + 1:]
    perm = tuple(range(axis)) + tuple(axis + o for o in order) + tuple(range(axis + k, len(full)))
    return jnp.transpose(w.reshape(full), perm).reshape(w.shape)


def _q_layout(w):
    return _transpose_cols(w, (2, GQA_HEADS // 2, 2, 2, 16), (1, 3, 0, 2, 4))


def _k_layout(w):
    return _transpose_cols(w, (2, 2, 2, 16), (2, 0, 1, 3))


def _o_layout(w, axis=-1):
    return _transpose_cols(w, (2, GQA_HEADS // 2, HEAD_D), (1, 0, 2), axis=axis)


def _permute_w_in(w_in):
    parts = [w_in[..., :O_GQ], _q_layout(w_in[..., O_GQ:O_GK]), _k_layout(w_in[..., O_GK:O_GV]),
             w_in[..., O_GV:O_GG], _o_layout(w_in[..., O_GG:O_MG])]
    return jnp.concatenate(parts, axis=-1)


def _rope_tables(n, t_total):
    _, d = _lane_dims()
    quarter = HEAD_D // 4
    freqs = ROPE_THETA ** (-jnp.arange(quarter, dtype=F32) / quarter)
    t = jnp.arange(n)
    pos_r = (t // GRID_W).astype(F32)
    pos_c = (t % GRID_W).astype(F32)
    dd = d % 32
    fi = dd % quarter
    use_col = (d // 32) == 1
    first = dd < quarter
    ang = jnp.where(use_col[None, :], pos_c[:, None], pos_r[:, None]) * freqs[fi][None, :]
    cos = jnp.cos(ang)
    sin = jnp.where(first[None, :], -jnp.sin(ang), jnp.sin(ang))
    pad = t_total - n
    cos = jnp.concatenate([cos, jnp.ones((pad, LANES), F32)], axis=0)
    sin = jnp.concatenate([sin, jnp.zeros((pad, LANES), F32)], axis=0)
    return cos, sin


def _ada_kernel(c_ref, w_ref, b_ref, o_ref):
    cs = _silu(c_ref[...])
    o_ref[0] = jnp.dot(cs, w_ref[0], preferred_element_type=F32) + b_ref[0]


def _ada_call(c_rows, ada_w, ada_b):
    depth = ada_w.shape[0]
    rows = c_rows.shape[0]
    nb = 3
    return pl.pallas_call(
        _ada_kernel,
        grid=(depth, nb),
        in_specs=[
            pl.BlockSpec((rows, D_MODEL), lambda l, j: (0, 0)),
            pl.BlockSpec((1, D_MODEL, D_MODEL), lambda l, j: (l, 0, j)),
            pl.BlockSpec((1, 1, D_MODEL), lambda l, j: (l, 0, j)),
        ],
        out_specs=pl.BlockSpec((1, rows, D_MODEL), lambda l, j: (l, 0, j)),
        out_shape=jax.ShapeDtypeStruct((depth, rows, nb * D_MODEL), F32),
        compiler_params=_params(("parallel", "parallel")),
        name="ada_mod",
    )(c_rows, ada_w, ada_b.reshape(depth, 1, nb * D_MODEL))


def _head_mean_sq(t, mask_a):
    sq = t * t
    s_all = jnp.sum(sq, axis=-1, keepdims=True)
    s_a = jnp.sum(jnp.where(mask_a, sq, 0.0), axis=-1, keepdims=True)
    return jnp.where(mask_a, s_a, s_all - s_a) * (1.0 / HEAD_D)


def _norm_rope(t, w_lane, cos, sin, mask_a):
    tn = t * lax.rsqrt(_head_mean_sq(t, mask_a) + EPS) * w_lane
    return tn * cos + pltpu.roll(tn, LANES // 2, 1) * sin


def _inproj_kernel(x_ref, mod_ref, nw_ref, w_ref, cos_ref, sin_ref, qw_ref, kw_ref,
                   rq_ref, rk_ref, rv_ref, rg_ref, nq_ref, nk_ref, nv_ref, ng_ref,
                   gq_ref, gk_ref, gv_ref, gg_ref, h_ref):
    x = x_ref[0]
    ms = jnp.mean(x * x, axis=-1, keepdims=True)
    y = x * lax.rsqrt(ms + EPS) * nw_ref[...]
    mod = mod_ref[0]
    shift = mod[:, :D_MODEL]
    scale = mod[:, D_MODEL:2 * D_MODEL]
    h = (y * (1.0 + scale) + shift).astype(BF16)

    def proj(off, width):
        return jnp.dot(h, w_ref[:, off:off + width], preferred_element_type=F32)

    qk_scale = HEAD_D ** -0.5
    rq_ref[0] = proj(O_RQ, RET_W).astype(BF16)
    rk_ref[0] = proj(O_RK, RET_W).astype(BF16)
    rv_ref[0] = proj(O_RV, RET_W).astype(BF16)
    rg_ref[0] = _silu(proj(O_RG, RET_W)).astype(BF16)
    nq_ref[0] = (proj(O_NQ, NA_W) * (qk_scale * LOG2E)).astype(BF16)
    nk_ref[0] = proj(O_NK, NA_W).astype(BF16)
    nv_ref[0] = proj(O_NV, NA_W).astype(BF16)
    ng_ref[0] = _silu(proj(O_NG, NA_W)).astype(BF16)

    lane = lax.broadcasted_iota(jnp.int32, (1, LANES), 1)
    mask_a = (lane % 64) < 32
    cos = cos_ref[...]
    sin = sin_ref[...]
    qw = qw_ref[...]
    for j2 in range(GQA_W // MXU_COLS):
        t2 = proj(O_GQ + j2 * MXU_COLS, MXU_COLS)
        for j in range(2 * j2, 2 * j2 + 2):
            t = t2[:, (j % 2) * LANES:(j % 2 + 1) * LANES]
            gq_ref[0, :, j * LANES:(j + 1) * LANES] = (_norm_rope(t, qw, cos, sin, mask_a)
                                                       * (qk_scale * LOG2E)).astype(BF16)
    kv = proj(O_GK, 2 * GQA_KV_W)
    gk_ref[0] = _norm_rope(kv[:, :GQA_KV_W], kw_ref[...], cos, sin, mask_a).astype(BF16)
    gv_ref[0] = kv[:, GQA_KV_W:].astype(BF16)
    gg_ref[0] = _silu(proj(O_GG, GQA_W)).astype(BF16)
    h_ref[0] = h


def _mod_index(n_lat_blocks, batch):
    return lambda b, i: (jnp.where(i < n_lat_blocks, b, batch), 0, 0)


def _inproj_call(xa, mod, norm_w, w_in, cos, sin, qw, kw, n_lat):
    b, t, _ = xa.shape
    tm = ROW_BLOCK
    widths = (RET_W, RET_W, RET_W, RET_W, NA_W, NA_W, NA_W, NA_W, GQA_W, GQA_KV_W, GQA_KV_W, GQA_W, D_MODEL)
    tok = lambda w: pl.BlockSpec((1, tm, w), lambda bi, i: (bi, i, 0))
    const2 = lambda shape: pl.BlockSpec(shape, lambda bi, i: (0, 0))
    return pl.pallas_call(
        _inproj_kernel,
        grid=(b, t // tm),
        in_specs=[
            tok(D_MODEL),
            pl.BlockSpec((1, 1, 3 * D_MODEL), _mod_index(n_lat // tm, b)),
            const2((1, D_MODEL)),
            pl.BlockSpec((D_MODEL, O_MG), lambda bi, i: (0, 0), pipeline_mode=pl.Buffered(1)),
            pl.BlockSpec((tm, LANES), lambda bi, i: (i, 0)),
            pl.BlockSpec((tm, LANES), lambda bi, i: (i, 0)),
            const2((1, LANES)),
            const2((1, LANES)),
        ],
        out_specs=[tok(w) for w in widths],
        out_shape=[jax.ShapeDtypeStruct((b, t, w), BF16) for w in widths],
        compiler_params=_params(("parallel", "parallel")),
        name="in_proj",
    )(xa, mod, norm_w, w_in, cos, sin, qw, kw)


def _ret_kernel(rld_ref, qf_ref, kf_ref, vf_ref, qb_ref, kb_ref, vb_ref, of_ref, ob_ref,
                state, dec, xi, zeta, gch):
    s = pl.program_id(1)
    c = RET_STEP
    ksc = RET_D ** -0.5

    @pl.when(s == 0)
    def _():
        state[...] = jnp.zeros_like(state)
        ii = lax.broadcasted_iota(jnp.int32, (c, c), 0)
        jj = lax.broadcasted_iota(jnp.int32, (c, c), 1)
        ri = lax.broadcasted_iota(jnp.int32, (c, RET_D), 0)
        for d in range(2):
            delta = ((ii - jj) if d == 0 else (jj - ii)).astype(F32)
            ip = (ri if d == 0 else c - 1 - ri).astype(F32)
            for h in range(RET_HEADS):
                log_g = -jnp.exp(jnp.full((1, 1), rld_ref[d, h], F32))
                dec[d, h] = jnp.where(delta >= 0, jnp.exp(jnp.maximum(delta, 0.0) * log_g), 0.0) * ksc
                xi[d, h] = jnp.exp((ip + 1.0) * log_g)
                zeta[d, h] = jnp.exp((c - 1.0 - ip) * log_g) * ksc
                gch[d, h] = jnp.exp(jnp.full((RET_D, RET_D), float(c), F32) * log_g)

    for d, (q_ref, k_ref, v_ref, o_ref) in enumerate(((qf_ref, kf_ref, vf_ref, of_ref),
                                                      (qb_ref, kb_ref, vb_ref, ob_ref))):
        for h in range(RET_HEADS):
            sl = slice(h * RET_D, (h + 1) * RET_D)
            q = q_ref[0, :, sl]
            k = k_ref[0, :, sl]
            v = v_ref[0, :, sl]
            scores = lax.dot_general(q, k, (((1,), (1,)), ((), ())), preferred_element_type=F32)
            intra = jnp.dot((scores * dec[d, h]).astype(BF16), v, preferred_element_type=F32)
            st = state[d, h]
            inter = jnp.dot(q, st.astype(BF16), preferred_element_type=F32) * xi[d, h]
            o_ref[0, :, sl] = (intra + inter).astype(o_ref.dtype)
            vz = (v.astype(F32) * zeta[d, h]).astype(BF16)
            upd = lax.dot_general(k, vz, (((0,), (0,)), ((), ())), preferred_element_type=F32)
            state[d, h] = gch[d, h] * st + upd


def _ret_call(rld, rq, rk, rv, n_lat):
    b, t, _ = rq.shape
    c = RET_STEP
    nc = t // c
    ncn = n_lat // c
    fwd = pl.BlockSpec((1, c, RET_W), lambda bi, s: (bi, (s + ncn) % nc, 0))
    bwd = pl.BlockSpec((1, c, RET_W), lambda bi, s: (bi, nc - 1 - s, 0))
    per_head = lambda rows, cols: pltpu.VMEM((2, RET_HEADS, rows, cols), F32)
    out = jax.ShapeDtypeStruct((b, t, RET_W), BF16)
    return pl.pallas_call(
        _ret_kernel,
        grid=(b, nc),
        in_specs=[pl.BlockSpec(memory_space=pltpu.SMEM), fwd, fwd, fwd, bwd, bwd, bwd],
        out_specs=[fwd, bwd],
        out_shape=[out, out],
        scratch_shapes=[per_head(RET_D, RET_D), per_head(c, c), per_head(c, RET_D), per_head(c, RET_D),
                        per_head(RET_D, RET_D)],
        compiler_params=_params(("parallel", "arbitrary")),
        name="retention",
    )(rld, rq, rk, rv, rq, rk, rv)


def _pair_split(q, mask_a):
    zero = jnp.zeros_like(q)
    return jnp.concatenate([jnp.where(mask_a, q, zero), jnp.where(mask_a, zero, q)], axis=0)


def _dot_t(a, b):
    return lax.dot_general(a, b, (((1,), (1,)), ((), ())), preferred_element_type=F32)


def _na_kernel(idx_ref, q_ref, k0_ref, k1_ref, k2_ref, v0_ref, v1_ref, v2_ref, kc_ref, vc_ref, bias_ref,
               g_ref, o_ref, *, g_units):
    u = ROW_BLOCK
    n_ctx = kc_ref.shape[1]
    g = pl.program_id(1)
    case = jnp.where(g == 0, 0, jnp.where(g < g_units - 1, 1, jnp.where(g == g_units - 1, 2, 3)))
    tile_of = [[idx_ref[case, rho, kp] for kp in range(NA_WIN_UNITS * NA_UNIT_ROWS // 2)]
               for rho in range(NA_UNIT_ROWS)]
    lane = lax.broadcasted_iota(jnp.int32, (1, LANES), 1)
    lo = lane < 64
    pairs = NA_W // LANES
    slices = [slice(j * LANES, (j + 1) * LANES) for j in range(pairs)]
    raw = []
    for sl in slices:
        q = q_ref[0, :, sl]
        k_all = jnp.concatenate([r[0, :, sl] for r in (kc_ref, k0_ref, k1_ref, k2_ref)], axis=0)
        zero = jnp.zeros_like(q)
        raw.append([_dot_t(jnp.where(lo, q, zero), k_all), _dot_t(jnp.where(lo, zero, q), k_all)])
    for j, sl in enumerate(slices):
        v_all = jnp.concatenate([r[0, :, sl] for r in (vc_ref, v0_ref, v1_ref, v2_ref)], axis=0)
        outs = []
        for x in range(2):

            def bias(i):
                return jnp.concatenate(
                    [jnp.concatenate([bias_ref[j, x, tile_of[rho][2 * i]], bias_ref[j, x, tile_of[rho][2 * i + 1]]],
                                     axis=1) for rho in range(NA_UNIT_ROWS)], axis=0)

            s = raw[j][x]
            s = jnp.concatenate([s[:, :n_ctx]] + [s[:, n_ctx + i * u:n_ctx + (i + 1) * u] + bias(i)
                                                  for i in range(NA_WIN_UNITS)], axis=1)
            p = jnp.exp2(s - jnp.max(s, axis=-1, keepdims=True))
            den = jnp.sum(p, axis=-1, keepdims=True)
            outs.append(jnp.dot(p.astype(BF16), v_all, preferred_element_type=F32) / den)
        o = jnp.where(lo, outs[0], outs[1])
        o_ref[0, :, sl] = (o * g_ref[0, :, sl].astype(F32)).astype(BF16)


def _na_bias_tables(rpb, rows):
    ur = NA_UNIT_ROWS
    g_units = rows // ur
    n_dr = 2 * NA_WIN_ROWS - 1
    n_dc = 2 * NA_WIN_COLS - 1
    qc = np.arange(GRID_W)
    cstart = np.clip(qc - NA_WIN_COLS // 2, 0, GRID_W - NA_WIN_COLS)
    col_ok = (qc[None, :] >= cstart[:, None]) & (qc[None, :] < cstart[:, None] + NA_WIN_COLS)
    dc = qc[None, :] - qc[:, None] + NA_WIN_COLS - 1
    hot_c = ((dc[None] == np.arange(n_dc)[:, None, None]) & col_ok[None]).astype(np.float32)
    hi = lax.Precision.HIGHEST
    toep = jnp.einsum('lhrd,dqj->lhrqj', rpb.astype(F32) * LOG2E, jnp.asarray(hot_c), precision=hi)
    toep = jnp.where(jnp.asarray(col_ok), toep, MASKED)
    toep = jnp.concatenate([toep, jnp.full_like(toep[:, :, :1], MASKED)], axis=2)

    combos = {(n_dr, n_dr): 0}
    index = np.zeros((4, ur, NA_WIN_UNITS * ur // 2), np.int32)
    for case, g in enumerate((0, 1, g_units - 1)):
        ws = int(np.clip(g - 1, 0, g_units - NA_WIN_UNITS)) * ur
        for rho in range(ur):
            r = g * ur + rho
            rs = int(np.clip(r - NA_WIN_ROWS // 2, 0, rows - NA_WIN_ROWS))
            for kp in range(NA_WIN_UNITS * ur // 2):
                pair = []
                for kr in (ws + 2 * kp, ws + 2 * kp + 1):
                    ok = rs <= kr < rs + NA_WIN_ROWS
                    pair.append(kr - r + NA_WIN_ROWS - 1 if ok else n_dr)
                index[case, rho, kp] = combos.setdefault(tuple(pair), len(combos))
    left = np.array([c[0] for c in combos], np.int32)
    right = np.array([c[1] for c in combos], np.int32)
    tiles = jnp.concatenate([jnp.take(toep, left, axis=2), jnp.take(toep, right, axis=2)], axis=-1)
    depth = rpb.shape[0]
    tiles = tiles.reshape(depth, NA_HEADS // 2, 2, len(combos), GRID_W, 2 * GRID_W)
    return tiles, jnp.asarray(index)


def _na_call(nq, nk, nv, ng, tiles, index, n_lat, n_ctx, with_ctx):
    b, t, _ = nq.shape
    u = ROW_BLOCK
    g_units = n_lat // u
    n_units = (t if with_ctx else n_lat) // u

    def win(i):
        return lambda bi, g: (bi, jnp.clip(g - 1, 0, g_units - NA_WIN_UNITS) + i, 0)

    tok = pl.BlockSpec((1, u, NA_W), lambda bi, g: (bi, g, 0))
    ctx = pl.BlockSpec((1, n_ctx, NA_W), lambda bi, g: (bi, n_lat // n_ctx, 0))
    kv = [pl.BlockSpec((1, u, NA_W), win(i)) for i in range(NA_WIN_UNITS)]
    return pl.pallas_call(
        functools.partial(_na_kernel, g_units=g_units),
        grid=(b, n_units),
        in_specs=[pl.BlockSpec(memory_space=pltpu.SMEM), tok] + kv + kv
        + [ctx, ctx, pl.BlockSpec(tiles.shape, lambda bi, g: (0,) * tiles.ndim), tok],
        out_specs=tok,
        out_shape=jax.ShapeDtypeStruct((b, n_units * u, NA_W), BF16),
        compiler_params=_params(("parallel", "arbitrary")),
        name="nbr_attn",
    )(index, nq, nk, nk, nk, nv, nv, nv, nk, nv, tiles, ng)


def _gqa_kernel(bound_ref, q_ref, k_ref, v_ref, g_ref, o_ref, q2_sc, m_sc, l_sc, acc_sc, *,
                n_lat, tk_shift, tk_online):
    i = pl.program_id(1)
    tq = q_ref.shape[1]
    pairs = GQA_W // LANES
    lane = lax.broadcasted_iota(jnp.int32, (1, LANES), 1)
    mask_a = (lane % 64) < 32
    lo = lane < 64
    for j in range(pairs):
        q2_sc[j] = _pair_split(q_ref[0, :, j * LANES:(j + 1) * LANES], mask_a)
    t_all = k_ref.shape[1]
    is_latent = i * tq < n_lat
    bound = bound_ref[0]

    def over_keys(step, tk):
        @pl.when(is_latent)
        def _():
            def body(c, carry):
                off = pl.multiple_of(c * tk, tk)
                step(k_ref[0, pl.ds(off, tk), :], v_ref[0, pl.ds(off, tk), :])
                return carry
            lax.fori_loop(0, n_lat // tk, body, 0)
        step(k_ref[0, n_lat:t_all, :], v_ref[0, n_lat:t_all, :])

    def write_out(j, o2):
        sl = slice(j * LANES, (j + 1) * LANES)
        o = jnp.where(lo, o2[:tq], o2[tq:])
        o_ref[0, :, sl] = (o * g_ref[0, :, sl].astype(F32)).astype(BF16)

    @pl.when(bound <= GQA_SHIFT_LIMIT)
    def _():
        l_sc[...] = jnp.zeros_like(l_sc)
        acc_sc[...] = jnp.zeros_like(acc_sc)

        def step(k, v):
            width = k.shape[0]
            for j in range(pairs):
                p = jnp.exp2(_dot_t(q2_sc[j], k) - bound)
                part = p[:, :LANES]
                for cidx in range(1, width // LANES):
                    part = part + p[:, cidx * LANES:(cidx + 1) * LANES]
                l_sc[j] = l_sc[j] + part
                acc_sc[j] = acc_sc[j] + jnp.dot(p.astype(BF16), v, preferred_element_type=F32)

        over_keys(step, tk_shift)
        for j in range(pairs):
            write_out(j, acc_sc[j] / jnp.sum(l_sc[j], axis=-1, keepdims=True))

    @pl.when(jnp.logical_not(bound <= GQA_SHIFT_LIMIT))
    def _():
        m_sc[...] = jnp.full_like(m_sc, MASKED)
        l_sc[...] = jnp.zeros_like(l_sc)
        acc_sc[...] = jnp.zeros_like(acc_sc)

        def step(k, v):
            width = k.shape[0]
            for j in range(pairs):
                s = _dot_t(q2_sc[j], k)
                m_old = m_sc[j]
                m_new = jnp.maximum(m_old, jnp.max(s, axis=-1, keepdims=True))
                alpha = jnp.exp2(m_old - m_new)
                p = jnp.exp2(s - jnp.concatenate([m_new] * (width // LANES), axis=1))
                l_sc[j] = alpha * l_sc[j] + jnp.sum(p, axis=-1, keepdims=True)
                acc_sc[j] = alpha * acc_sc[j] + jnp.dot(p.astype(BF16), v, preferred_element_type=F32)
                m_sc[j] = m_new

        over_keys(step, tk_online)
        for j in range(pairs):
            write_out(j, acc_sc[j] / l_sc[j])


def _gqa_score_bound(q_norm_w, k_norm_w):
    bound = HEAD_D * jnp.max(jnp.abs(q_norm_w)) * jnp.max(jnp.abs(k_norm_w)) * (HEAD_D ** -0.5 * LOG2E)
    return (bound * 1.02).reshape(1).astype(F32)


def _gqa_call(bound, gq, gk, gv, gg, n_lat, with_ctx):
    b, t, _ = gq.shape
    tq = ROW_BLOCK
    t_out = t if with_ctx else n_lat
    pairs = GQA_W // LANES
    tok = pl.BlockSpec((1, tq, GQA_W), lambda bi, i: (bi, i, 0))
    kv = pl.BlockSpec((1, t, GQA_KV_W), lambda bi, i: (bi, 0, 0))
    return pl.pallas_call(
        functools.partial(_gqa_kernel, n_lat=n_lat, tk_shift=min(GQA_KEY_CHUNK, n_lat),
                          tk_online=min(GQA_ONLINE_KEY_CHUNK, n_lat)),
        grid=(b, t_out // tq),
        in_specs=[pl.BlockSpec(memory_space=pltpu.SMEM), tok, kv, kv, tok],
        out_specs=tok,
        out_shape=jax.ShapeDtypeStruct((b, t_out, GQA_W), BF16),
        scratch_shapes=[pltpu.VMEM((pairs, 2 * tq, LANES), BF16)] + [pltpu.VMEM((pairs, 2 * tq, LANES), F32)] * 3,
        compiler_params=_params(("parallel", "arbitrary")),
        name="gqa_attn",
    )(bound, gq, gk, gv, gg)


def _merge_kernel(x_ref, mod_ref, of_ref, ob_ref, rg_ref, gnw_ref, na_ref, gqa_ref, h_ref,
                  wm_ref, wr_ref, wn_ref, wg_ref, wo_ref, fw_ref, o_ref, *, final):
    def branch_gate(j):
        return _sigmoid(jnp.dot(h_ref[0], wm_ref[:, j * D_MODEL:(j + 1) * D_MODEL], preferred_element_type=F32))

    o = of_ref[0].astype(F32) + ob_ref[0].astype(F32)
    parts = []
    for h in range(RET_HEADS):
        oh = o[:, h * RET_D:(h + 1) * RET_D]
        mu = jnp.mean(oh, axis=-1, keepdims=True)
        dev = oh - mu
        var = jnp.mean(dev * dev, axis=-1, keepdims=True)
        parts.append(dev * lax.rsqrt(var + EPS))
    ret_x = (jnp.concatenate(parts, axis=-1) * gnw_ref[...] * rg_ref[0].astype(F32)).astype(BF16)
    m = branch_gate(0) * jnp.dot(ret_x, wr_ref[...], preferred_element_type=F32)
    m = m + branch_gate(1) * jnp.dot(na_ref[0], wn_ref[...], preferred_element_type=F32)
    m = m + branch_gate(2) * jnp.dot(gqa_ref[0], wg_ref[...], preferred_element_type=F32)
    out = jnp.dot(m.astype(BF16), wo_ref[...], preferred_element_type=F32)
    gate = mod_ref[0][:, 2 * D_MODEL:]
    xn = x_ref[0] + gate * out
    if final:
        ms = jnp.mean(xn * xn, axis=-1, keepdims=True)
        xn = xn * lax.rsqrt(ms + EPS) * fw_ref[...]
    o_ref[0] = xn


def _merge_call(xa, mod, ret_o, rg, gnw, na_o, gqa_o, h, wm, wr, wn, wg, wo, fw, n_lat, final):
    b, t, _ = xa.shape
    tm = ROW_BLOCK
    t_out = n_lat if final else t
    tok = lambda w: pl.BlockSpec((1, tm, w), lambda bi, i: (bi, i, 0))
    const2 = lambda shape: pl.BlockSpec(shape, lambda bi, i: (0, 0))
    return pl.pallas_call(
        functools.partial(_merge_kernel, final=final),
        grid=(b, t_out // tm),
        in_specs=[
            tok(D_MODEL),
            pl.BlockSpec((1, 1, 3 * D_MODEL), _mod_index(n_lat // tm, b)),
            tok(RET_W),
            tok(RET_W),
            tok(RET_W),
            const2((1, RET_W)),
            tok(NA_W),
            tok(GQA_W),
            tok(D_MODEL),
            const2((D_MODEL, N_BRANCH * D_MODEL)),
            const2((RET_W, D_MODEL)),
            const2((NA_W, D_MODEL)),
            const2((GQA_W, D_MODEL)),
            const2((D_MODEL, D_MODEL)),
            const2((1, D_MODEL)),
        ],
        out_specs=tok(D_MODEL),
        out_shape=jax.ShapeDtypeStruct((b, t_out, D_MODEL), F32),
        compiler_params=_params(("parallel", "parallel")),
        name="merge",
    )(xa, mod, ret_o[0], ret_o[1], rg, gnw, na_o, gqa_o, h, wm, wr, wn, wg, wo, fw)


def _forward(x, c, ctx, c_ctx, ada_w, ada_b, norm_w, w_in, ret_log_decay, ret_gn_w, na_rpb,
             q_norm_w, k_norm_w, w_ret_o, w_na_o, w_gqa_o, w_out, final_norm_w):
    b, n, _ = x.shape
    n_ctx = ctx.shape[1]
    depth = ada_w.shape[0]
    t = n + n_ctx
    rows = n // GRID_W
    assert n % ROW_BLOCK == 0 and n_ctx == ROW_BLOCK and n % n_ctx == 0
    assert n % min(GQA_KEY_CHUNK, n) == 0 and n % min(GQA_ONLINE_KEY_CHUNK, n) == 0
    assert rows >= NA_WIN_UNITS * NA_UNIT_ROWS and rows % NA_UNIT_ROWS == 0

    w_in_p = _permute_w_in(w_in).astype(BF16)
    w_mg_p = w_in[..., O_MG:].astype(BF16)
    qw =_k_layout(jnp.concatenate([q_norm_w, q_norm_w], axis=-1)).reshape(depth, 1, LANES)
    kw = _k_layout(jnp.concatenate([k_norm_w, k_norm_w], axis=-1)).reshape(depth, 1, LANES)
    w_ret_p = w_ret_o.astype(BF16)
    w_na_p = w_na_o.astype(BF16)
    w_gqa_p = _o_layout(w_gqa_o, axis=-2).astype(BF16)
    w_out_p = w_out.astype(BF16)
    na_tiles, na_index = _na_bias_tables(na_rpb, rows)

    cos, sin = _rope_tables(n, t)
    mod_rows = 8 * ((b + 1 + 7) // 8)
    c_rows = jnp.concatenate([c, c_ctx[None, :], jnp.zeros((mod_rows - b - 1, D_MODEL), F32)], axis=0)
    mod_all = _ada_call(c_rows, ada_w, ada_b)

    xa = jnp.concatenate([x, ctx], axis=1)
    for layer in range(depth):
        final = layer == depth - 1
        mod = mod_all[layer].reshape(mod_rows, 1, 3 * D_MODEL)
        (rq, rk, rv, rg, nq, nk, nv, ng, gq, gk, gv, gg, h) = _inproj_call(
            xa, mod, norm_w[layer].reshape(1, D_MODEL), w_in_p[layer], cos, sin, qw[layer], kw[layer], n)
        ret_o = _ret_call(ret_log_decay[layer], rq, rk, rv, n)
        na_o = _na_call(nq, nk, nv, ng, na_tiles[layer], na_index, n, n_ctx, with_ctx=not final)
        gqa_o = _gqa_call(_gqa_score_bound(q_norm_w[layer], k_norm_w[layer]), gq, gk, gv, gg, n,
                          with_ctx=not final)
        xa = _merge_call(xa, mod, ret_o, rg, ret_gn_w[layer].reshape(1, RET_W), na_o, gqa_o, h,
                         w_mg_p[layer], w_ret_p[layer], w_na_p[layer], w_gqa_p[layer], w_out_p[layer],
                         final_norm_w.reshape(1, D_MODEL), n, final)
    return xa


def kernel(x, c, ctx, c_ctx, ada_w, ada_b, norm_w, w_in, ret_log_decay, ret_gn_w, na_rpb, q_norm_w, k_norm_w,
           w_ret_o, w_na_o, w_gqa_o, w_out, final_norm_w):
    return _forward(x, c, ctx, c_ctx, ada_w, ada_b, norm_w, w_in, ret_log_decay, ret_gn_w, na_rpb,
                    q_norm_w, k_norm_w, w_ret_o, w_na_o, w_gqa_o, w_out, final_norm_w)
```
